```python
import jax, jax.numpy as jnp
from jax import lax
import numpy as np

D_MODEL = 2048
BATCH = 16
SEQ = 256
DEPTH = 4
DEC_BATCH = 4
DEC_SEQ = 1024
PAST_LEN = 512

GRID_W = 64
HEAD_DIM = 128
N_Q_HEADS = 8
N_KV_HEADS = 2
Q_PER_KV = N_Q_HEADS // N_KV_HEADS
ATTN_WIDTH = N_Q_HEADS * HEAD_DIM
KV_WIDTH = N_KV_HEADS * HEAD_DIM
CONV_WIDTH = D_MODEL - ATTN_WIDTH
MIX_WIDTH = ATTN_WIDTH + CONV_WIDTH
IN_COLS = ATTN_WIDTH + 2 * KV_WIDTH + 2 * CONV_WIDTH
CONV_KERNEL = 31
CONV_PAD = CONV_KERNEL // 2
D_FF = 4 * D_MODEL
N_MOD = 6
Q_BLOCK = 128
ROPE_THETA = 10000.0
EPS = 1e-6

kernel_name = 'hybrid_gqa_conformer_diffusion_step'


def rms_norm(x, g):
    xf = x.astype(jnp.float32)
    y = xf * lax.rsqrt(jnp.mean(xf * xf, axis=-1, keepdims=True) + EPS)
    return (y * g.astype(jnp.float32)).astype(x.dtype)


def layer_norm(x, g, b):
    xf = x.astype(jnp.float32)
    mu = jnp.mean(xf, axis=-1, keepdims=True)
    xc = xf - mu
    y = xc * lax.rsqrt(jnp.mean(xc * xc, axis=-1, keepdims=True) + EPS)
    return (y * g.astype(jnp.float32) + b.astype(jnp.float32)).astype(x.dtype)


def grid_rotary(n_tokens):
    n_rows = n_tokens // GRID_W
    row = jnp.repeat(jnp.arange(n_rows, dtype=jnp.int32), GRID_W).astype(jnp.float32)
    col = jnp.tile(jnp.arange(GRID_W, dtype=jnp.int32), n_rows).astype(jnp.float32)
    n_pairs_axis = HEAD_DIM // 4
    freqs = ROPE_THETA ** (-jnp.arange(n_pairs_axis, dtype=jnp.float32) / n_pairs_axis)
    ang = jnp.concatenate([row[:, None] * freqs, col[:, None] * freqs], axis=-1)
    return jnp.cos(ang), jnp.sin(ang)


def apply_rotary(x, cos, sin):
    xf = x.astype(jnp.float32)
    half = HEAD_DIM // 2
    x1, x2 = xf[..., :half], xf[..., half:]
    cb, sb = cos[None, :, None, :], sin[None, :, None, :]
    return jnp.concatenate([x1 * cb - x2 * sb, x2 * cb + x1 * sb], axis=-1).astype(x.dtype)


def block_attention(q, k, v):
    b, tq = q.shape[0], q.shape[1]
    nb = tq // Q_BLOCK
    qb = q.reshape(b, nb, Q_BLOCK, N_KV_HEADS, Q_PER_KV, HEAD_DIM).transpose(1, 0, 2, 3, 4, 5)
    scale = HEAD_DIM ** -0.5

    def one_block(q_blk):
        s = jnp.einsum('bqkgd,bskd->bkgqs', q_blk, k).astype(jnp.float32) * scale
        p = jax.nn.softmax(s, axis=-1).astype(v.dtype)
        return jnp.einsum('bkgqs,bskd->bqkgd', p, v)

    o = lax.map(one_block, qb)
    return o.transpose(1, 0, 2, 3, 4, 5).reshape(b, tq, ATTN_WIDTH)


def depthwise_conv(u, w, bias):
    y = lax.conv_general_dilated(u, w[:, None, :], window_strides=(1,), padding=[(CONV_PAD, CONV_PAD)],
                                 dimension_numbers=('NWC', 'WIO', 'NWC'), feature_group_count=CONV_WIDTH)
    return y + bias


def trunk_layer(x, cond, rotary, ctx_k, ctx_v,
                norm1_g, w_mod, b_mod, w_in, q_norm_g, k_norm_g, conv_w, conv_b,
                conv_norm_g, conv_norm_b, w_out, norm2_g, w_ff1, w_ff2):
    b, t = x.shape[0], x.shape[1]
    mod = jax.nn.silu(cond) @ w_mod + b_mod
    shift1, scale1, gate1, shift2, scale2, gate2 = [m[:, None, :] for m in jnp.split(mod, N_MOD, axis=-1)]

    h = rms_norm(x, norm1_g) * (1 + scale1) + shift1
    proj = h @ w_in
    q, k, v, u = jnp.split(proj, [ATTN_WIDTH, ATTN_WIDTH + KV_WIDTH, ATTN_WIDTH + 2 * KV_WIDTH], axis=-1)
    q = rms_norm(q.reshape(b, t, N_Q_HEADS, HEAD_DIM), q_norm_g)
    k = rms_norm(k.reshape(b, t, N_KV_HEADS, HEAD_DIM), k_norm_g)
    v = v.reshape(b, t, N_KV_HEADS, HEAD_DIM)

    if rotary is None:
        attn = block_attention(q, k, v)
    else:
        cos, sin = rotary
        q_r = apply_rotary(q, cos, sin)
        k_r = apply_rotary(k, cos, sin)
        keys = jnp.concatenate([k_r, ctx_k], axis=1)
        vals = jnp.concatenate([v, ctx_v], axis=1)
        attn = block_attention(q_r, keys, vals)

    glu = u[..., :CONV_WIDTH] * jax.nn.sigmoid(u[..., CONV_WIDTH:])
    cv = jax.nn.silu(layer_norm(depthwise_conv(glu, conv_w, conv_b), conv_norm_g, conv_norm_b))

    mix = jnp.concatenate([attn, cv], axis=-1) @ w_out
    x = x + gate1 * mix

    h2 = rms_norm(x, norm2_g) * (1 + scale2) + shift2
    ff = jnp.square(jax.nn.relu(h2 @ w_ff1)) @ w_ff2
    x = x + gate2 * ff
    return x, k, v


def setup_inputs(seed: int = 0) -> dict:
    key = jax.random.key(seed)
    ks = jax.random.split(key, 24)
    f32 = jnp.float32
    nrm = lambda k, shape, s: jax.random.normal(k, shape, f32) * s
    return {
        'x_prompt': nrm(ks[0], (BATCH, SEQ, D_MODEL), 1.0),
        'x_sample': nrm(ks[1], (DEC_BATCH, DEC_SEQ, D_MODEL), 1.0),
        'cache_k': nrm(ks[2], (DEC_BATCH, DEPTH, PAST_LEN, N_KV_HEADS, HEAD_DIM), 1.0),
        'cache_v': nrm(ks[3], (DEC_BATCH, DEPTH, PAST_LEN, N_KV_HEADS, HEAD_DIM), 1.0),
        'c': nrm(ks[4], (DEC_BATCH, D_MODEL), 1.0),
        'c_ctx': nrm(ks[5], (D_MODEL,), 1.0),
        'norm1_g': 1.0 + nrm(ks[6], (DEPTH, D_MODEL), 0.02),
        'w_mod': nrm(ks[7], (DEPTH, D_MODEL, N_MOD * D_MODEL), 0.5 * D_MODEL ** -0.5),
        'b_mod': nrm(ks[8], (DEPTH, N_MOD * D_MODEL), 0.02),
        'w_in': nrm(ks[9], (DEPTH, D_MODEL, IN_COLS), D_MODEL ** -0.5),
        'q_norm_g': 1.0 + nrm(ks[10], (DEPTH, HEAD_DIM), 0.02),
        'k_norm_g': 1.0 + nrm(ks[11], (DEPTH, HEAD_DIM), 0.02),
        'conv_w': nrm(ks[12], (DEPTH, CONV_KERNEL, CONV_WIDTH), CONV_KERNEL ** -0.5),
        'conv_b': nrm(ks[13], (DEPTH, CONV_WIDTH), 0.02),
        'conv_norm_g': 1.0 + nrm(ks[14], (DEPTH, CONV_WIDTH), 0.02),
        'conv_norm_b': nrm(ks[15], (DEPTH, CONV_WIDTH), 0.02),
        'w_out': nrm(ks[16], (DEPTH, MIX_WIDTH, D_MODEL), MIX_WIDTH ** -0.5),
        'norm2_g': 1.0 + nrm(ks[17], (DEPTH, D_MODEL), 0.02),
        'w_ff1': nrm(ks[18], (DEPTH, D_MODEL, D_FF), D_MODEL ** -0.5),
        'w_ff2': nrm(ks[19], (DEPTH, D_FF, D_MODEL), D_FF ** -0.5),
    }


def reference(x_prompt, x_sample, cache_k, cache_v, c, c_ctx,
              norm1_g, w_mod, b_mod, w_in, q_norm_g, k_norm_g, conv_w, conv_b,
              conv_norm_g, conv_norm_b, w_out, norm2_g, w_ff1, w_ff2):
    rotary = grid_rotary(x_sample.shape[1])
    cond_ctx = c_ctx[None, :]
    y_prompt = x_prompt
    y_sample = x_sample
    ks_new, vs_new = [], []
    for l in range(DEPTH):
        p = (norm1_g[l], w_mod[l], b_mod[l], w_in[l], q_norm_g[l], k_norm_g[l], conv_w[l], conv_b[l],
             conv_norm_g[l], conv_norm_b[l], w_out[l], norm2_g[l], w_ff1[l], w_ff2[l])
        y_prompt, k_l, v_l = trunk_layer(y_prompt, cond_ctx, None, None, None, *p)
        ks_new.append(k_l)
        vs_new.append(v_l)
        y_sample, _, _ = trunk_layer(y_sample, c, rotary, cache_k[:, l], cache_v[:, l], *p)
    new_k = jnp.stack(ks_new, axis=1)
    new_v = jnp.stack(vs_new, axis=1)
    return (y_prompt, y_sample, new_k, new_v)
```

```python
import functools

import jax
import jax.numpy as jnp
from jax import lax
from jax.experimental import pallas as pl
from jax.experimental.pallas import tpu as pltpu

F32 = jnp.float32
BF16 = jnp.bfloat16

HEAD_DIM = 128
N_Q_HEADS = 8
N_KV_HEADS = 2
Q_PER_KV = N_Q_HEADS // N_KV_HEADS
GRID_W = 64
CONV_KERNEL = 31
CONV_PAD = CONV_KERNEL // 2
N_MOD = 6
ROPE_THETA = 10000.0
EPS = 1e-6

LANES = 128
CONV_HALO = 16
CONV_TILE = 256
CONV_CHUNK = 32
MIB = 1024 * 1024


def _params(semantics, vmem_mib):
    return pltpu.CompilerParams(dimension_semantics=semantics, vmem_limit_bytes=vmem_mib * MIB)


def _mod_kernel(c_ref, w_ref, b_ref, o_ref):
    a = jax.nn.silu(c_ref[...]).astype(BF16)
    o_ref[...] = jnp.dot(a, w_ref[...].astype(BF16), preferred_element_type=F32) + b_ref[...]


def _modulation(cond8, w_mod, b_mod, tn=1024):
    depth, d, n = w_mod.shape
    return pl.pallas_call(
        _mod_kernel,
        grid=(depth, n // tn),
        in_specs=[
            pl.BlockSpec((8, d), lambda l, j: (0, 0)),
            pl.BlockSpec((None, d, tn), lambda l, j: (l, 0, j)),
            pl.BlockSpec((None, 1, tn), lambda l, j: (l, 0, j)),
        ],
        out_specs=pl.BlockSpec((None, 8, tn), lambda l, j: (l, 0, j)),
        out_shape=jax.ShapeDtypeStruct((depth, 8, n), F32),
        compiler_params=_params(("arbitrary", "arbitrary"), 40),
        name="modulation",
    )(cond8, w_mod, b_mod.reshape(depth, 1, n))


def _rms_heads(acc, g, n_heads, rotary):
    outs = []
    for h in range(n_heads):
        y = acc[:, h * HEAD_DIM:(h + 1) * HEAD_DIM]
        y = y * lax.rsqrt(jnp.mean(y * y, axis=-1, keepdims=True) + EPS) * g
        if rotary is not None:
            cos2, sin2 = rotary
            y = y * cos2 + pltpu.roll(y, HEAD_DIM // 2, 1) * sin2
        outs.append(y)
    return outs


def _in_kernel(x_ref, mod_ref, g1_ref, w_ref, qg_ref, kg_ref, cos_ref, sin_ref,
               q_ref, kv_ref, u_ref, h_ref, *, n_ctx_tiles, tn):
    i = pl.program_id(0)
    j = pl.program_id(1)
    n_q_tiles = N_Q_HEADS * HEAD_DIM // tn
    heads_per_tile = tn // HEAD_DIM

    @pl.when(j == 0)
    def _():
        x = x_ref[...]
        y = x * lax.rsqrt(jnp.mean(x * x, axis=-1, keepdims=True) + EPS) * g1_ref[...]
        h_ref[...] = (y * (1.0 + mod_ref[1:2, :]) + mod_ref[0:1, :]).astype(BF16)

    acc = jnp.dot(h_ref[...], w_ref[...], preferred_element_type=F32)
    is_ctx = i < n_ctx_tiles

    def write_q(rotary):
        ys = _rms_heads(acc, qg_ref[...], heads_per_tile, rotary)
        for h, y in enumerate(ys):
            q_ref[:, h * HEAD_DIM:(h + 1) * HEAD_DIM] = y.astype(BF16)

    def write_kv(rotary):
        kw = N_KV_HEADS * HEAD_DIM
        ys = _rms_heads(acc[:, :kw], kg_ref[...], N_KV_HEADS, rotary)
        for h, y in enumerate(ys):
            kv_ref[:, h * HEAD_DIM:(h + 1) * HEAD_DIM] = y
        kv_ref[:, kw:] = acc[:, kw:]

    @pl.when(jnp.logical_and(j < n_q_tiles, is_ctx))
    def _():
        write_q(None)

    @pl.when(jnp.logical_and(j < n_q_tiles, jnp.logical_not(is_ctx)))
    def _():
        write_q((cos_ref[...], sin_ref[...]))

    @pl.when(jnp.logical_and(j == n_q_tiles, is_ctx))
    def _():
        write_kv(None)

    @pl.when(jnp.logical_and(j == n_q_tiles, jnp.logical_not(is_ctx)))
    def _():
        write_kv((cos_ref[...], sin_ref[...]))

    @pl.when(j > n_q_tiles)
    def _():
        u_ref[...] = acc


def _in_projection(x, mod, g1, w_in, qg, kg, cos2, sin2, *, n_ctx, seq_s, tm=512, tn=512):
    n_tok, d = x.shape
    n_cols = w_in.shape[1]
    attn_w = N_Q_HEADS * HEAD_DIM
    kv_w = 2 * N_KV_HEADS * HEAD_DIM
    assert tn == kv_w and attn_w % tn == 0 and n_ctx % tm == 0 and seq_s % tm == 0
    conv2 = n_cols - attn_w - kv_w
    n_q_tiles = attn_w // tn
    n_u_tiles = conv2 // tn
    n_ctx_tiles = n_ctx // tm
    tiles_per_seq = seq_s // tm

    def mod_row(i):
        return jnp.where(i < n_ctx_tiles, 0, 1 + (i - n_ctx_tiles) // tiles_per_seq)

    kern = functools.partial(_in_kernel, n_ctx_tiles=n_ctx_tiles, tn=tn)
    return pl.pallas_call(
        kern,
        grid=(n_tok // tm, n_cols // tn),
        in_specs=[
            pl.BlockSpec((tm, d), lambda i, j: (i, 0)),
            pl.BlockSpec((None, N_MOD, d), lambda i, j: (mod_row(i), 0, 0)),
            pl.BlockSpec((1, d), lambda i, j: (0, 0)),
            pl.BlockSpec((d, tn), lambda i, j: (0, j)),
            pl.BlockSpec((1, HEAD_DIM), lambda i, j: (0, 0)),
            pl.BlockSpec((1, HEAD_DIM), lambda i, j: (0, 0)),
            pl.BlockSpec((tm, HEAD_DIM), lambda i, j: (i % tiles_per_seq, 0)),
            pl.BlockSpec((tm, HEAD_DIM), lambda i, j: (i % tiles_per_seq, 0)),
        ],
        out_specs=[
            pl.BlockSpec((tm, tn), lambda i, j: (i, jnp.minimum(j, n_q_tiles - 1))),
            pl.BlockSpec((tm, tn), lambda i, j: (i, 0)),
            pl.BlockSpec((tm, tn), lambda i, j: (i, jnp.clip(j - n_q_tiles - 1, 0, n_u_tiles - 1))),
        ],
        out_shape=[
            jax.ShapeDtypeStruct((n_tok, attn_w), BF16),
            jax.ShapeDtypeStruct((n_tok, kv_w), F32),
            jax.ShapeDtypeStruct((n_tok, conv2), F32),
        ],
        scratch_shapes=[pltpu.VMEM((tm, d), BF16)],
        compiler_params=_params(("arbitrary", "arbitrary"), 40),
        name="in_projection",
    )(x, mod, g1, w_in, qg, kg, cos2, sin2)


def _softmax_pv(q, keys, vals, scale):
    dn = (((1,), (1,)), ((), ()))
    ss = [lax.dot_general(q, k, dn, preferred_element_type=F32) for k in keys]
    m = ss[0].max(axis=-1, keepdims=True)
    for s in ss[1:]:
        m = jnp.maximum(m, s.max(axis=-1, keepdims=True))
    es = [jnp.exp((s - m) * scale) for s in ss]
    den = es[0].sum(axis=-1, keepdims=True)
    for e in es[1:]:
        den = den + e.sum(axis=-1, keepdims=True)
    o = jnp.dot(es[0].astype(BF16), vals[0], preferred_element_type=F32)
    for e, v in zip(es[1:], vals[1:]):
        o = o + jnp.dot(e.astype(BF16), v, preferred_element_type=F32)
    return o / den


def _attn_ctx_kernel(q_ref, k_ref, v_ref, o_ref):
    scale = HEAD_DIM ** -0.5
    k = k_ref[...].astype(BF16)
    v = v_ref[...].astype(BF16)
    for g in range(Q_PER_KV):
        sl = slice(g * HEAD_DIM, (g + 1) * HEAD_DIM)
        o_ref[:, sl] = _softmax_pv(q_ref[:, sl], [k], [v], scale).astype(BF16)


def _attn_sample_kernel(q_ref, k_ref, v_ref, ck_ref, cv_ref, o_ref):
    scale = HEAD_DIM ** -0.5
    keys = [k_ref[...].astype(BF16), ck_ref[...].astype(BF16)]
    vals = [v_ref[...].astype(BF16), cv_ref[...].astype(BF16)]
    for g in range(Q_PER_KV):
        sl = slice(g * HEAD_DIM, (g + 1) * HEAD_DIM)
        o_ref[:, sl] = _softmax_pv(q_ref[:, sl], keys, vals, scale).astype(BF16)


def _attention_ctx(q, kv, *, n_seq, seq):
    gw = Q_PER_KV * HEAD_DIM
    return pl.pallas_call(
        _attn_ctx_kernel,
        grid=(n_seq, N_KV_HEADS),
        in_specs=[
            pl.BlockSpec((seq, gw), lambda b, h: (b, h)),
            pl.BlockSpec((seq, HEAD_DIM), lambda b, h: (b, h)),
            pl.BlockSpec((seq, HEAD_DIM), lambda b, h: (b, N_KV_HEADS + h)),
        ],
        out_specs=pl.BlockSpec((seq, gw), lambda b, h: (b, h)),
        out_shape=jax.ShapeDtypeStruct((n_seq * seq, N_Q_HEADS * HEAD_DIM), BF16),
        compiler_params=_params(("arbitrary", "arbitrary"), 32),
        name="attention_ctx",
    )(q, kv, kv)


def _attention_sample(q, kv, cache_k, cache_v, layer, *, n_ctx, n_seq, seq, tq=256):
    gw = Q_PER_KV * HEAD_DIM
    past = cache_k.shape[2]
    q0 = n_ctx // tq
    s0 = n_ctx // seq
    n_qt = seq // tq
    return pl.pallas_call(
        _attn_sample_kernel,
        grid=(n_seq, N_KV_HEADS, n_qt),
        in_specs=[
            pl.BlockSpec((tq, gw), lambda b, h, t: (q0 + b * n_qt + t, h)),
            pl.BlockSpec((seq, HEAD_DIM), lambda b, h, t: (s0 + b, h)),
            pl.BlockSpec((seq, HEAD_DIM), lambda b, h, t: (s0 + b, N_KV_HEADS + h)),
            pl.BlockSpec((None, None, past, HEAD_DIM), lambda b, h, t: (b, layer, 0, h)),
            pl.BlockSpec((None, None, past, HEAD_DIM), lambda b, h, t: (b, layer, 0, h)),
        ],
        out_specs=pl.BlockSpec((tq, gw), lambda b, h, t: (b * n_qt + t, h)),
        out_shape=jax.ShapeDtypeStruct((n_seq * seq, N_Q_HEADS * HEAD_DIM), BF16),
        compiler_params=_params(("arbitrary", "arbitrary", "arbitrary"), 32),
        name="attention_sample",
    )(q, kv, kv, cache_k, cache_v)


def _conv_kernel(u1_ref, u2_ref, p1_ref, p2_ref, n1_ref, n2_ref, w_ref, b_ref, g_ref, beta_ref,
                 o_ref, pad_ref, cv_ref, *, n_ctx_tiles, tiles_per_seq):
    i = pl.program_id(0)
    n_lt = pad_ref.shape[0]
    t = CONV_TILE
    pos = (i - n_ctx_tiles) % tiles_per_seq
    is_smp = i >= n_ctx_tiles
    has_prev = jnp.logical_and(is_smp, pos != 0)
    has_next = jnp.logical_and(is_smp, pos != tiles_per_seq - 1)

    for c in range(n_lt):
        sl = slice(c * LANES, (c + 1) * LANES)
        prev = p1_ref[:, sl] * jax.nn.sigmoid(p2_ref[:, sl])
        nxt = n1_ref[:, sl] * jax.nn.sigmoid(n2_ref[:, sl])
        pad_ref[c, 0:CONV_HALO, :] = jnp.where(has_prev, prev, 0.0)
        pad_ref[c, CONV_HALO:CONV_HALO + t, :] = u1_ref[:, sl] * jax.nn.sigmoid(u2_ref[:, sl])
        pad_ref[c, CONV_HALO + t:, :] = jnp.where(has_next, nxt, 0.0)

    def lane_tile(c, carry):
        for r0 in range(0, t, CONV_CHUNK):
            acc = jnp.zeros((CONV_CHUNK, LANES), F32)
            for k in range(CONV_KERNEL):
                start = r0 + CONV_HALO - CONV_PAD + k
                acc = acc + pad_ref[c, start:start + CONV_CHUNK, :] * w_ref[c, k:k + 1, :]
            cv_ref[c, r0:r0 + CONV_CHUNK, :] = acc + b_ref[c]
        return carry

    lax.fori_loop(0, n_lt, lane_tile, 0)

    width = n_lt * LANES
    tot = cv_ref[0]
    for c in range(1, n_lt):
        tot = tot + cv_ref[c]
    mu = tot.sum(axis=-1, keepdims=True) / width
    sq = jnp.zeros_like(tot)
    for c in range(n_lt):
        d = cv_ref[c] - mu
        sq = sq + d * d
    rstd = lax.rsqrt(sq.sum(axis=-1, keepdims=True) / width + EPS)
    for c in range(n_lt):
        z = (cv_ref[c] - mu) * rstd * g_ref[c] + beta_ref[c]
        o_ref[:, c * LANES:(c + 1) * LANES] = (z * jax.nn.sigmoid(z)).astype(BF16)


def _conv_group(u, conv_w, conv_b, ln_g, ln_b, *, n_ctx, seq_c, seq_s):
    n_tok = u.shape[0]
    width = u.shape[1] // 2
    n_lt = width // LANES
    t = CONV_TILE
    assert seq_c == t and seq_s % t == 0 and n_ctx % t == 0
    n_ctx_tiles = n_ctx // t
    tiles_per_seq = seq_s // t
    hb = t // CONV_HALO
    last_hb = n_tok // CONV_HALO - 1
    lane_major = lambda a: a.reshape(-1, n_lt, LANES).transpose(1, 0, 2)
    kern = functools.partial(_conv_kernel, n_ctx_tiles=n_ctx_tiles, tiles_per_seq=tiles_per_seq)
    prev_map = lambda c: (lambda i: (jnp.maximum(i * hb - 1, 0), c))
    next_map = lambda c: (lambda i: (jnp.minimum((i + 1) * hb, last_hb), c))
    vec_spec = pl.BlockSpec((n_lt, 1, LANES), lambda i: (0, 0, 0))
    return pl.pallas_call(
        kern,
        grid=(n_tok // t,),
        in_specs=[
            pl.BlockSpec((t, width), lambda i: (i, 0)),
            pl.BlockSpec((t, width), lambda i: (i, 1)),
            pl.BlockSpec((CONV_HALO, width), prev_map(0)),
            pl.BlockSpec((CONV_HALO, width), prev_map(1)),
            pl.BlockSpec((CONV_HALO, width), next_map(0)),
            pl.BlockSpec((CONV_HALO, width), next_map(1)),
            pl.BlockSpec((n_lt, CONV_KERNEL, LANES), lambda i: (0, 0, 0)),
            vec_spec, vec_spec, vec_spec,
        ],
        out_specs=pl.BlockSpec((t, width), lambda i: (i, 0)),
        out_shape=jax.ShapeDtypeStruct((n_tok, width), BF16),
        scratch_shapes=[
            pltpu.VMEM((n_lt, t + 2 * CONV_HALO, LANES), F32),
            pltpu.VMEM((n_lt, t, LANES), F32),
        ],
        compiler_params=_params(("arbitrary",), 32),
        name="conv_group",
    )(u, u, u, u, u, u, lane_major(conv_w), lane_major(conv_b[None]), lane_major(ln_g[None]),
      lane_major(ln_b[None]))


def _out_kernel(a_ref, c_ref, x_ref, mod_ref, g2_ref, w_ref, xo_ref, h_ref):
    ka = a_ref.shape[1]
    mix = jnp.dot(a_ref[...], w_ref[0:ka, :], preferred_element_type=F32)
    mix = mix + jnp.dot(c_ref[...], w_ref[ka:, :], preferred_element_type=F32)
    x = x_ref[...] + mod_ref[2:3, :] * mix
    xo_ref[...] = x
    y = x * lax.rsqrt(jnp.mean(x * x, axis=-1, keepdims=True) + EPS) * g2_ref[...]
    h_ref[...] = (y * (1.0 + mod_ref[4:5, :]) + mod_ref[3:4, :]).astype(BF16)


def _out_projection(attn, cv, x, mod, g2, w_out, *, n_ctx, seq_s, tm=512):
    n_tok, d = x.shape
    ka, kc = attn.shape[1], cv.shape[1]
    n_ctx_tiles = n_ctx // tm
    tiles_per_seq = seq_s // tm

    def mod_row(i):
        return jnp.where(i < n_ctx_tiles, 0, 1 + (i - n_ctx_tiles) // tiles_per_seq)

    return pl.pallas_call(
        _out_kernel,
        grid=(n_tok // tm,),
        in_specs=[
            pl.BlockSpec((tm, ka), lambda i: (i, 0)),
            pl.BlockSpec((tm, kc), lambda i: (i, 0)),
            pl.BlockSpec((tm, d), lambda i: (i, 0)),
            pl.BlockSpec((None, N_MOD, d), lambda i: (mod_row(i), 0, 0)),
            pl.BlockSpec((1, d), lambda i: (0, 0)),
            pl.BlockSpec((ka + kc, d), lambda i: (0, 0)),
        ],
        out_specs=[
            pl.BlockSpec((tm, d), lambda i: (i, 0)),
            pl.BlockSpec((tm, d), lambda i: (i, 0)),
        ],
        out_shape=[
            jax.ShapeDtypeStruct((n_tok, d), F32),
            jax.ShapeDtypeStruct((n_tok, d), BF16),
        ],
        compiler_params=_params(("arbitrary",), 52),
        name="out_projection",
    )(attn, cv, x, mod, g2, w_out)


def _ffn_kernel(h_ref, x_ref, mod_ref, w1_ref, w2_ref, o_ref):
    k = pl.program_id(1)
    a = jnp.dot(h_ref[...], w1_ref[...], preferred_element_type=F32)
    a = jnp.square(jnp.maximum(a, 0.0)).astype(BF16)
    p = jnp.dot(a, w2_ref[...], preferred_element_type=F32)

    @pl.when(k == 0)
    def _():
        o_ref[...] = p

    @pl.when(k > 0)
    def _():
        o_ref[...] += p

    @pl.when(k == pl.num_programs(1) - 1)
    def _():
        o_ref[...] = x_ref[...] + mod_ref[5:6, :] * o_ref[...]


def _ffn(h, x, mod, w1, w2, *, n_ctx, seq_s, tm=512, tf=512):
    n_tok, d = x.shape
    d_ff = w1.shape[1]
    n_ctx_tiles = n_ctx // tm
    tiles_per_seq = seq_s // tm

    def mod_row(i):
        return jnp.where(i < n_ctx_tiles, 0, 1 + (i - n_ctx_tiles) // tiles_per_seq)

    return pl.pallas_call(
        _ffn_kernel,
        grid=(n_tok // tm, d_ff // tf),
        in_specs=[
            pl.BlockSpec((tm, d), lambda i, k: (i, 0)),
            pl.BlockSpec((tm, d), lambda i, k: (i, 0)),
            pl.BlockSpec((None, N_MOD, d), lambda i, k: (mod_row(i), 0, 0)),
            pl.BlockSpec((d, tf), lambda i, k: (0, k)),
            pl.BlockSpec((tf, d), lambda i, k: (k, 0)),
        ],
        out_specs=pl.BlockSpec((tm, d), lambda i, k: (i, 0)),
        out_shape=jax.ShapeDtypeStruct((n_tok, d), F32),
        compiler_params=_params(("arbitrary", "arbitrary"), 48),
        name="ffn",
    )(h, x, mod, w1, w2)


def _rotary_tables(n_tokens):
    n_rows = n_tokens // GRID_W
    row = jnp.repeat(jnp.arange(n_rows, dtype=jnp.int32), GRID_W).astype(F32)
    col = jnp.tile(jnp.arange(GRID_W, dtype=jnp.int32), n_rows).astype(F32)
    n_pairs_axis = HEAD_DIM // 4
    freqs = ROPE_THETA ** (-jnp.arange(n_pairs_axis, dtype=F32) / n_pairs_axis)
    ang = jnp.concatenate([row[:, None] * freqs, col[:, None] * freqs], axis=-1)
    cos, sin = jnp.cos(ang), jnp.sin(ang)
    return jnp.concatenate([cos, cos], axis=-1), jnp.concatenate([-sin, sin], axis=-1)


def kernel(x_prompt, x_sample, cache_k, cache_v, c, c_ctx, norm1_g, w_mod, b_mod, w_in, q_norm_g, k_norm_g,
           conv_w, conv_b, conv_norm_g, conv_norm_b, w_out, norm2_g, w_ff1, w_ff2):
    n_cseq, seq_c, d = x_prompt.shape
    n_sseq, seq_s, _ = x_sample.shape
    depth = w_in.shape[0]
    n_ctx = n_cseq * seq_c
    n_smp = n_sseq * seq_s
    kvw = N_KV_HEADS * HEAD_DIM

    x = jnp.concatenate([x_prompt.reshape(n_ctx, d), x_sample.reshape(n_smp, d)], axis=0)
    cond8 = jnp.concatenate([c_ctx[None, :], c, jnp.zeros((8 - 1 - n_sseq, d), F32)], axis=0)
    mod = _modulation(cond8, w_mod, b_mod).reshape(depth, 8, N_MOD, d)
    cos2, sin2 = _rotary_tables(seq_s)
    ck = cache_k.reshape(n_sseq, depth, cache_k.shape[2], kvw)
    cvv = cache_v.reshape(n_sseq, depth, cache_v.shape[2], kvw)
    dims = dict(n_ctx=n_ctx, seq_s=seq_s)

    new_k, new_v = [], []
    for l in range(depth):
        q, kv, u = _in_projection(x, mod[l], norm1_g[l][None], w_in[l].astype(BF16), q_norm_g[l][None],
                                  k_norm_g[l][None], cos2, sin2, **dims)
        new_k.append(kv[:n_ctx, :kvw].reshape(n_cseq, seq_c, N_KV_HEADS, HEAD_DIM))
        new_v.append(kv[:n_ctx, kvw:].reshape(n_cseq, seq_c, N_KV_HEADS, HEAD_DIM))
        attn = jnp.concatenate([
            _attention_ctx(q, kv, n_seq=n_cseq, seq=seq_c),
            _attention_sample(q, kv, ck, cvv, l, n_ctx=n_ctx, n_seq=n_sseq, seq=seq_s),
        ], axis=0)
        cv = _conv_group(u, conv_w[l], conv_b[l], conv_norm_g[l], conv_norm_b[l],
                         n_ctx=n_ctx, seq_c=seq_c, seq_s=seq_s)
        x, h2 = _out_projection(attn, cv, x, mod[l], norm2_g[l][None], w_out[l].astype(BF16), **dims)
        x = _ffn(h2, x, mod[l], w_ff1[l].astype(BF16), w_ff2[l].astype(BF16), **dims)

    y_prompt = x[:n_ctx].reshape(n_cseq, seq_c, d)
    y_sample = x[n_ctx:].reshape(n_sseq, seq_s, d)
    return (y_prompt, y_sample, jnp.stack(new_k, axis=1), jnp.stack(new_v, axis=1))
```

```python
import functools

import jax
import jax.numpy as jnp
from jax import lax
from jax.experimental import pallas as pl
from jax.experimental.pallas import tpu as pltpu

F32 = jnp.float32
BF16 = jnp.bfloat16

HEAD_DIM = 128
N_Q_HEADS = 8
N_KV_HEADS = 2
Q_PER_KV = N_Q_HEADS // N_KV_HEADS
GRID_W = 64
CONV_KERNEL = 31
CONV_PAD = CONV_KERNEL // 2
N_MOD = 6
ROPE_THETA = 10000.0
EPS = 1e-6

LANES = 128
CONV_HALO = 16
CONV_TILE = 256
CONV_CHUNK = 32
MIB = 1024 * 1024


def _params(semantics, vmem_mib):
    return pltpu.CompilerParams(dimension_semantics=semantics, vmem_limit_bytes=vmem_mib * MIB)


def _mod_kernel(c_ref, w_ref, b_ref, o_ref):
    a = jax.nn.silu(c_ref[...]).astype(BF16)
    o_ref[...] = jnp.dot(a, w_ref[...].astype(BF16), preferred_element_type=F32) + b_ref[...]


def _modulation(cond8, w_mod, b_mod, tn=1024):
    depth, d, n = w_mod.shape
    return pl.pallas_call(
        _mod_kernel,
        grid=(depth, n // tn),
        in_specs=[
            pl.BlockSpec((8, d), lambda l, j: (0, 0)),
            pl.BlockSpec((None, d, tn), lambda l, j: (l, 0, j)),
            pl.BlockSpec((None, 1, tn), lambda l, j: (l, 0, j)),
        ],
        out_specs=pl.BlockSpec((None, 8, tn), lambda l, j: (l, 0, j)),
        out_shape=jax.ShapeDtypeStruct((depth, 8, n), F32),
        compiler_params=_params(("arbitrary", "arbitrary"), 40),
        name="modulation",
    )(cond8, w_mod, b_mod.reshape(depth, 1, n))


def _rms_heads(acc, g, n_heads, rotary):
    outs = []
    for h in range(n_heads):
        y = acc[:, h * HEAD_DIM:(h + 1) * HEAD_DIM]
        y = y * lax.rsqrt(jnp.mean(y * y, axis=-1, keepdims=True) + EPS) * g
        if rotary is not None:
            cos2, sin2 = rotary
            y = y * cos2 + pltpu.roll(y, HEAD_DIM // 2, 1) * sin2
        outs.append(y)
    return outs


def _in_kernel(x_ref, mod_ref, g1_ref, w_ref, qg_ref, kg_ref, cos_ref, sin_ref, nk_in, nv_in,
               q_ref, kv_ref, u_ref, nk_ref, nv_ref, h_ref, *, n_ctx_tiles, tn):
    del nk_in, nv_in
    i = pl.program_id(0)
    j = pl.program_id(1)
    n_q_tiles = N_Q_HEADS * HEAD_DIM // tn
    heads_per_tile = tn // HEAD_DIM

    @pl.when(j == 0)
    def _():
        x = x_ref[...]
        y = x * lax.rsqrt(jnp.mean(x * x, axis=-1, keepdims=True) + EPS) * g1_ref[...]
        h_ref[...] = (y * (1.0 + mod_ref[1:2, :]) + mod_ref[0:1, :]).astype(BF16)

    acc = jnp.dot(h_ref[...], w_ref[...].astype(BF16), preferred_element_type=F32)
    is_ctx = i < n_ctx_tiles

    def write_q(rotary):
        ys = _rms_heads(acc, qg_ref[...], heads_per_tile, rotary)
        for h, y in enumerate(ys):
            q_ref[:, h * HEAD_DIM:(h + 1) * HEAD_DIM] = y.astype(BF16)

    def write_kv(rotary):
        kw = N_KV_HEADS * HEAD_DIM
        ys = _rms_heads(acc[:, :kw], kg_ref[...], N_KV_HEADS, rotary)
        for h, y in enumerate(ys):
            kv_ref[:, h * HEAD_DIM:(h + 1) * HEAD_DIM] = y
        kv_ref[:, kw:] = acc[:, kw:]
        if rotary is None:
            seq = nk_ref.shape[1]
            for s in range(nk_ref.shape[0]):
                rows = slice(s * seq, (s + 1) * seq)
                for h, y in enumerate(ys):
                    nk_ref[s, :, h * HEAD_DIM:(h + 1) * HEAD_DIM] = y[rows]
                nv_ref[s] = acc[rows, kw:]

    @pl.when(jnp.logical_and(j < n_q_tiles, is_ctx))
    def _():
        write_q(None)

    @pl.when(jnp.logical_and(j < n_q_tiles, jnp.logical_not(is_ctx)))
    def _():
        write_q((cos_ref[...], sin_ref[...]))

    @pl.when(jnp.logical_and(j == n_q_tiles, is_ctx))
    def _():
        write_kv(None)

    @pl.when(jnp.logical_and(j == n_q_tiles, jnp.logical_not(is_ctx)))
    def _():
        write_kv((cos_ref[...], sin_ref[...]))

    @pl.when(j > n_q_tiles)
    def _():
        u_ref[...] = acc


def _in_projection(x, mod, g1, w_in, qg, kg, cos2, sin2, new_k, new_v, layer, *, n_ctx, seq_s, tm=1024, tn=512):
    n_tok, d = x.shape
    seq_c = new_k.shape[2]
    assert tm % seq_c == 0
    nkv_spec = pl.BlockSpec((tm // seq_c, None, seq_c, new_k.shape[3]),
                            lambda i, j: (jnp.minimum(i, n_ctx // tm - 1), layer, 0, 0))
    n_cols = w_in.shape[1]
    attn_w = N_Q_HEADS * HEAD_DIM
    kv_w = 2 * N_KV_HEADS * HEAD_DIM
    assert tn == kv_w and attn_w % tn == 0 and n_ctx % tm == 0 and seq_s % tm == 0
    conv2 = n_cols - attn_w - kv_w
    n_q_tiles = attn_w // tn
    n_u_tiles = conv2 // tn
    n_ctx_tiles = n_ctx // tm
    tiles_per_seq = seq_s // tm

    def mod_row(i):
        return jnp.where(i < n_ctx_tiles, 0, 1 + (i - n_ctx_tiles) // tiles_per_seq)

    kern = functools.partial(_in_kernel, n_ctx_tiles=n_ctx_tiles, tn=tn)
    return pl.pallas_call(
        kern,
        grid=(n_tok // tm, n_cols // tn),
        in_specs=[
            pl.BlockSpec((tm, d), lambda i, j: (i, 0)),
            pl.BlockSpec((None, N_MOD, d), lambda i, j: (mod_row(i), 0, 0)),
            pl.BlockSpec((1, d), lambda i, j: (0, 0)),
            pl.BlockSpec((d, tn), lambda i, j: (0, j)),
            pl.BlockSpec((1, HEAD_DIM), lambda i, j: (0, 0)),
            pl.BlockSpec((1, HEAD_DIM), lambda i, j: (0, 0)),
            pl.BlockSpec((tm, HEAD_DIM), lambda i, j: (i % tiles_per_seq, 0)),
            pl.BlockSpec((tm, HEAD_DIM), lambda i, j: (i % tiles_per_seq, 0)),
            pl.BlockSpec(memory_space=pl.ANY),
            pl.BlockSpec(memory_space=pl.ANY),
        ],
        out_specs=[
            pl.BlockSpec((tm, tn), lambda i, j: (i, jnp.minimum(j, n_q_tiles - 1))),
            pl.BlockSpec((tm, tn), lambda i, j: (i, 0)),
            pl.BlockSpec((tm, tn), lambda i, j: (i, jnp.clip(j - n_q_tiles - 1, 0, n_u_tiles - 1))),
            nkv_spec,
            nkv_spec,
        ],
        out_shape=[
            jax.ShapeDtypeStruct((n_tok, attn_w), BF16),
            jax.ShapeDtypeStruct((n_tok, kv_w), F32),
            jax.ShapeDtypeStruct((n_tok, conv2), F32),
            jax.ShapeDtypeStruct(new_k.shape, F32),
            jax.ShapeDtypeStruct(new_v.shape, F32),
        ],
        input_output_aliases={8: 3, 9: 4},
        scratch_shapes=[pltpu.VMEM((tm, d), BF16)],
        compiler_params=_params(("arbitrary", "arbitrary"), 56),
        name="in_projection",
    )(x, mod, g1, w_in, qg, kg, cos2, sin2, new_k, new_v)


def _softmax_pv(q, keys, vals, scale):
    dn = (((1,), (1,)), ((), ()))
    ss = [lax.dot_general(q, k, dn, preferred_element_type=F32) for k in keys]
    m = ss[0].max(axis=-1, keepdims=True)
    for s in ss[1:]:
        m = jnp.maximum(m, s.max(axis=-1, keepdims=True))
    es = [jnp.exp((s - m) * scale) for s in ss]
    den = es[0].sum(axis=-1, keepdims=True)
    for e in es[1:]:
        den = den + e.sum(axis=-1, keepdims=True)
    o = jnp.dot(es[0].astype(BF16), vals[0], preferred_element_type=F32)
    for e, v in zip(es[1:], vals[1:]):
        o = o + jnp.dot(e.astype(BF16), v, preferred_element_type=F32)
    return o / den


def _attn_ctx_kernel(q_ref, k_ref, v_ref, o_ref):
    scale = HEAD_DIM ** -0.5
    k = k_ref[...].astype(BF16)
    v = v_ref[...].astype(BF16)
    for g in range(Q_PER_KV):
        sl = slice(g * HEAD_DIM, (g + 1) * HEAD_DIM)
        o_ref[:, sl] = _softmax_pv(q_ref[:, sl], [k], [v], scale).astype(BF16)


def _attn_sample_kernel(q_ref, k_ref, v_ref, ck_ref, cv_ref, o_in, o_ref):
    del o_in
    scale = HEAD_DIM ** -0.5
    keys = [k_ref[...].astype(BF16), ck_ref[...].astype(BF16)]
    vals = [v_ref[...].astype(BF16), cv_ref[...].astype(BF16)]
    for g in range(Q_PER_KV):
        sl = slice(g * HEAD_DIM, (g + 1) * HEAD_DIM)
        o_ref[:, sl] = _softmax_pv(q_ref[:, sl], keys, vals, scale).astype(BF16)


def _attention_ctx(q, kv, *, n_seq, seq):
    gw = Q_PER_KV * HEAD_DIM
    return pl.pallas_call(
        _attn_ctx_kernel,
        grid=(n_seq, N_KV_HEADS),
        in_specs=[
            pl.BlockSpec((seq, gw), lambda b, h: (b, h)),
            pl.BlockSpec((seq, HEAD_DIM), lambda b, h: (b, h)),
            pl.BlockSpec((seq, HEAD_DIM), lambda b, h: (b, N_KV_HEADS + h)),
        ],
        out_specs=pl.BlockSpec((seq, gw), lambda b, h: (b, h)),
        out_shape=jax.ShapeDtypeStruct((q.shape[0], N_Q_HEADS * HEAD_DIM), BF16),
        compiler_params=_params(("arbitrary", "arbitrary"), 32),
        name="attention_ctx",
    )(q, kv, kv)


def _attention_sample(q, kv, cache_k, cache_v, attn, layer, *, n_ctx, n_seq, seq, tq=256):
    gw = Q_PER_KV * HEAD_DIM
    past = cache_k.shape[2]
    q0 = n_ctx // tq
    s0 = n_ctx // seq
    n_qt = seq // tq
    return pl.pallas_call(
        _attn_sample_kernel,
        grid=(n_seq, N_KV_HEADS, n_qt),
        in_specs=[
            pl.BlockSpec((tq, gw), lambda b, h, t: (q0 + b * n_qt + t, h)),
            pl.BlockSpec((seq, HEAD_DIM), lambda b, h, t: (s0 + b, h)),
            pl.BlockSpec((seq, HEAD_DIM), lambda b, h, t: (s0 + b, N_KV_HEADS + h)),
            pl.BlockSpec((None, None, past, HEAD_DIM), lambda b, h, t: (b, layer, 0, h)),
            pl.BlockSpec((None, None, past, HEAD_DIM), lambda b, h, t: (b, layer, 0, h)),
            pl.BlockSpec(memory_space=pl.ANY),
        ],
        out_specs=pl.BlockSpec((tq, gw), lambda b, h, t: (q0 + b * n_qt + t, h)),
        out_shape=jax.ShapeDtypeStruct(attn.shape, BF16),
        input_output_aliases={5: 0},
        compiler_params=_params(("arbitrary", "arbitrary", "arbitrary"), 32),
        name="attention_sample",
    )(q, kv, kv, cache_k, cache_v, attn)


def _conv_kernel(u1_ref, u2_ref, p1_ref, p2_ref, n1_ref, n2_ref, w_ref, b_ref, g_ref, beta_ref,
                 o_ref, pad_ref, cv_ref, *, n_ctx_tiles, tiles_per_seq):
    i = pl.program_id(0)
    n_lt = pad_ref.shape[0]
    t = CONV_TILE
    pos = (i - n_ctx_tiles) % tiles_per_seq
    is_smp = i >= n_ctx_tiles
    has_prev = jnp.logical_and(is_smp, pos != 0)
    has_next = jnp.logical_and(is_smp, pos != tiles_per_seq - 1)

    for c in range(n_lt):
        sl = slice(c * LANES, (c + 1) * LANES)
        prev = p1_ref[:, sl] * jax.nn.sigmoid(p2_ref[:, sl])
        nxt = n1_ref[:, sl] * jax.nn.sigmoid(n2_ref[:, sl])
        pad_ref[c, 0:CONV_HALO, :] = jnp.where(has_prev, prev, 0.0)
        pad_ref[c, CONV_HALO:CONV_HALO + t, :] = u1_ref[:, sl] * jax.nn.sigmoid(u2_ref[:, sl])
        pad_ref[c, CONV_HALO + t:, :] = jnp.where(has_next, nxt, 0.0)

    def lane_tile(c, carry):
        for r0 in range(0, t, CONV_CHUNK):
            acc = jnp.zeros((CONV_CHUNK, LANES), F32)
            for k in range(CONV_KERNEL):
                start = r0 + CONV_HALO - CONV_PAD + k
                acc = acc + pad_ref[c, start:start + CONV_CHUNK, :] * w_ref[c, k:k + 1, :]
            cv_ref[c, r0:r0 + CONV_CHUNK, :] = acc + b_ref[c]
        return carry

    lax.fori_loop(0, n_lt, lane_tile, 0)

    width = n_lt * LANES
    tot = cv_ref[0]
    for c in range(1, n_lt):
        tot = tot + cv_ref[c]
    mu = tot.sum(axis=-1, keepdims=True) / width
    sq = jnp.zeros_like(tot)
    for c in range(n_lt):
        d = cv_ref[c] - mu
        sq = sq + d * d
    rstd = lax.rsqrt(sq.sum(axis=-1, keepdims=True) / width + EPS)
    for c in range(n_lt):
        z = (cv_ref[c] - mu) * rstd * g_ref[c] + beta_ref[c]
        o_ref[:, c * LANES:(c + 1) * LANES] = (z * jax.nn.sigmoid(z)).astype(BF16)


def _conv_group(u, conv_w, conv_b, ln_g, ln_b, *, n_ctx, seq_c, seq_s):
    n_tok = u.shape[0]
    width = u.shape[1] // 2
    n_lt = width // LANES
    t = CONV_TILE
    assert seq_c == t and seq_s % t == 0 and n_ctx % t == 0
    n_ctx_tiles = n_ctx // t
    tiles_per_seq = seq_s // t
    hb = t // CONV_HALO
    last_hb = n_tok // CONV_HALO - 1
    lane_major = lambda a: a.reshape(-1, n_lt, LANES).transpose(1, 0, 2)
    kern = functools.partial(_conv_kernel, n_ctx_tiles=n_ctx_tiles, tiles_per_seq=tiles_per_seq)
    prev_map = lambda c: (lambda i: (jnp.maximum(i * hb - 1, 0), c))
    next_map = lambda c: (lambda i: (jnp.minimum((i + 1) * hb, last_hb), c))
    vec_spec = pl.BlockSpec((n_lt, 1, LANES), lambda i: (0, 0, 0))
    return pl.pallas_call(
        kern,
        grid=(n_tok // t,),
        in_specs=[
            pl.BlockSpec((t, width), lambda i: (i, 0)),
            pl.BlockSpec((t, width), lambda i: (i, 1)),
            pl.BlockSpec((CONV_HALO, width), prev_map(0)),
            pl.BlockSpec((CONV_HALO, width), prev_map(1)),
            pl.BlockSpec((CONV_HALO, width), next_map(0)),
            pl.BlockSpec((CONV_HALO, width), next_map(1)),
            pl.BlockSpec((n_lt, CONV_KERNEL, LANES), lambda i: (0, 0, 0)),
            vec_spec, vec_spec, vec_spec,
        ],
        out_specs=pl.BlockSpec((t, width), lambda i: (i, 0)),
        out_shape=jax.ShapeDtypeStruct((n_tok, width), BF16),
        scratch_shapes=[
            pltpu.VMEM((n_lt, t + 2 * CONV_HALO, LANES), F32),
            pltpu.VMEM((n_lt, t, LANES), F32),
        ],
        compiler_params=_params(("arbitrary",), 32),
        name="conv_group",
    )(u, u, u, u, u, u, lane_major(conv_w), lane_major(conv_b[None]), lane_major(ln_g[None]),
      lane_major(ln_b[None]))


def _out_kernel(a_ref, c_ref, x_ref, mod_ref, g2_ref, w_ref, xo_ref, h_ref):
    ka = a_ref.shape[1]
    mix = jnp.dot(a_ref[...], w_ref[0:ka, :], preferred_element_type=F32)
    mix = mix + jnp.dot(c_ref[...], w_ref[ka:, :], preferred_element_type=F32)
    x = x_ref[...] + mod_ref[2:3, :] * mix
    xo_ref[...] = x
    y = x * lax.rsqrt(jnp.mean(x * x, axis=-1, keepdims=True) + EPS) * g2_ref[...]
    h_ref[...] = (y * (1.0 + mod_ref[4:5, :]) + mod_ref[3:4, :]).astype(BF16)


def _out_projection(attn, cv, x, mod, g2, w_out, *, n_ctx, seq_s, tm=512):
    n_tok, d = x.shape
    ka, kc = attn.shape[1], cv.shape[1]
    n_ctx_tiles = n_ctx // tm
    tiles_per_seq = seq_s // tm

    def mod_row(i):
        return jnp.where(i < n_ctx_tiles, 0, 1 + (i - n_ctx_tiles) // tiles_per_seq)

    return pl.pallas_call(
        _out_kernel,
        grid=(n_tok // tm,),
        in_specs=[
            pl.BlockSpec((tm, ka), lambda i: (i, 0)),
            pl.BlockSpec((tm, kc), lambda i: (i, 0)),
            pl.BlockSpec((tm, d), lambda i: (i, 0)),
            pl.BlockSpec((None, N_MOD, d), lambda i: (mod_row(i), 0, 0)),
            pl.BlockSpec((1, d), lambda i: (0, 0)),
            pl.BlockSpec((ka + kc, d), lambda i: (0, 0)),
        ],
        out_specs=[
            pl.BlockSpec((tm, d), lambda i: (i, 0)),
            pl.BlockSpec((tm, d), lambda i: (i, 0)),
        ],
        out_shape=[
            jax.ShapeDtypeStruct((n_tok, d), F32),
            jax.ShapeDtypeStruct((n_tok, d), BF16),
        ],
        compiler_params=_params(("arbitrary",), 52),
        name="out_projection",
    )(attn, cv, x, mod, g2, w_out)


def _ffn_step(h_ref, x_ref, mod_ref, w1_ref, w2_ref, o_ref, *, tn):
    k = pl.program_id(1)
    d = o_ref.shape[1]

    @pl.when(k == 0)
    def _():
        o_ref[...] = jnp.zeros_like(o_ref)

    a = jnp.dot(h_ref[...], w1_ref[...], preferred_element_type=F32)
    a = jnp.square(jnp.maximum(a, 0.0)).astype(BF16)
    for n0 in range(0, d, tn):
        o_ref[:, n0:n0 + tn] += jnp.dot(a, w2_ref[:, n0:n0 + tn], preferred_element_type=F32)

    @pl.when(k == pl.num_programs(1) - 1)
    def _():
        o_ref[...] = x_ref[...] + mod_ref[5:6, :] * o_ref[...]


def _ffn_kernel(h_ref, x_ref, mod_ref, w1_ref, w2_ref, *o_refs, n_ctx_tiles, tn):
    step = functools.partial(_ffn_step, h_ref, x_ref, mod_ref, w1_ref, w2_ref, tn=tn)
    if len(o_refs) == 1:
        step(o_refs[0])
        return
    is_ctx = pl.program_id(0) < n_ctx_tiles
    pl.when(is_ctx)(lambda: step(o_refs[0]))
    pl.when(jnp.logical_not(is_ctx))(lambda: step(o_refs[1]))


def _ffn(h, x, mod, w1, w2, *, n_ctx, seq_s, split_streams, tm=1024, tf=512, tn=512):
    n_tok, d = x.shape
    d_ff = w1.shape[1]
    if split_streams:
        tm = tm // 2
    n_ctx_tiles = n_ctx // tm
    tiles_per_seq = seq_s // tm

    def mod_row(i):
        return jnp.where(i < n_ctx_tiles, 0, 1 + (i - n_ctx_tiles) // tiles_per_seq)

    if split_streams:
        out_specs = [
            pl.BlockSpec((tm, d), lambda i, k: (jnp.minimum(i, n_ctx_tiles - 1), 0)),
            pl.BlockSpec((tm, d), lambda i, k: (jnp.maximum(i - n_ctx_tiles, 0), 0)),
        ]
        out_shape = [jax.ShapeDtypeStruct((n_ctx, d), F32), jax.ShapeDtypeStruct((n_tok - n_ctx, d), F32)]
    else:
        out_specs = pl.BlockSpec((tm, d), lambda i, k: (i, 0))
        out_shape = jax.ShapeDtypeStruct((n_tok, d), F32)
    return pl.pallas_call(
        functools.partial(_ffn_kernel, n_ctx_tiles=n_ctx_tiles, tn=tn),
        grid=(n_tok // tm, d_ff // tf),
        in_specs=[
            pl.BlockSpec((tm, d), lambda i, k: (i, 0)),
            pl.BlockSpec((tm, d), lambda i, k: (i, 0)),
            pl.BlockSpec((None, N_MOD, d), lambda i, k: (mod_row(i), 0, 0)),
            pl.BlockSpec((d, tf), lambda i, k: (0, k)),
            pl.BlockSpec((tf, d), lambda i, k: (k, 0)),
        ],
        out_specs=out_specs,
        out_shape=out_shape,
        compiler_params=_params(("arbitrary", "arbitrary"), 58),
        name="ffn_split" if split_streams else "ffn",
    )(h, x, mod, w1, w2)


def _rotary_tables(n_tokens):
    n_rows = n_tokens // GRID_W
    row = jnp.repeat(jnp.arange(n_rows, dtype=jnp.int32), GRID_W).astype(F32)
    col = jnp.tile(jnp.arange(GRID_W, dtype=jnp.int32), n_rows).astype(F32)
    n_pairs_axis = HEAD_DIM // 4
    freqs = ROPE_THETA ** (-jnp.arange(n_pairs_axis, dtype=F32) / n_pairs_axis)
    ang = jnp.concatenate([row[:, None] * freqs, col[:, None] * freqs], axis=-1)
    cos, sin = jnp.cos(ang), jnp.sin(ang)
    return jnp.concatenate([cos, cos], axis=-1), jnp.concatenate([-sin, sin], axis=-1)


def kernel(x_prompt, x_sample, cache_k, cache_v, c, c_ctx, norm1_g, w_mod, b_mod, w_in, q_norm_g, k_norm_g,
           conv_w, conv_b, conv_norm_g, conv_norm_b, w_out, norm2_g, w_ff1, w_ff2):
    n_cseq, seq_c, d = x_prompt.shape
    n_sseq, seq_s, _ = x_sample.shape
    depth = w_in.shape[0]
    n_ctx = n_cseq * seq_c
    n_smp = n_sseq * seq_s
    kvw = N_KV_HEADS * HEAD_DIM

    x = jnp.concatenate([x_prompt.reshape(n_ctx, d), x_sample.reshape(n_smp, d)], axis=0)
    cond8 = jnp.concatenate([c_ctx[None, :], c, jnp.zeros((8 - 1 - n_sseq, d), F32)], axis=0)
    mod = _modulation(cond8, w_mod, b_mod).reshape(depth, 8, N_MOD, d)
    cos2, sin2 = _rotary_tables(seq_s)
    ck = cache_k.reshape(n_sseq, depth, cache_k.shape[2], kvw)
    cvv = cache_v.reshape(n_sseq, depth, cache_v.shape[2], kvw)
    dims = dict(n_ctx=n_ctx, seq_s=seq_s)

    new_k = jnp.zeros((n_cseq, depth, seq_c, kvw), F32)
    new_v = jnp.zeros((n_cseq, depth, seq_c, kvw), F32)
    for l in range(depth):
        q, kv, u, new_k, new_v = _in_projection(
            x, mod[l], norm1_g[l][None], w_in[l], q_norm_g[l][None], k_norm_g[l][None],
            cos2, sin2, new_k, new_v, l, **dims)
        attn = _attention_ctx(q, kv, n_seq=n_cseq, seq=seq_c)
        attn = _attention_sample(q, kv, ck, cvv, attn, l, n_ctx=n_ctx, n_seq=n_sseq, seq=seq_s)
        cv = _conv_group(u, conv_w[l], conv_b[l], conv_norm_g[l], conv_norm_b[l],
                         n_ctx=n_ctx, seq_c=seq_c, seq_s=seq_s)
        x, h2 = _out_projection(attn, cv, x, mod[l], norm2_g[l][None], w_out[l].astype(BF16), **dims)
        x = _ffn(h2, x, mod[l], w_ff1[l].astype(BF16), w_ff2[l].astype(BF16),
                 split_streams=(l == depth - 1), **dims)

    y_prompt, y_sample = x
    kv_shape = (n_cseq, depth, seq_c, N_KV_HEADS, HEAD_DIM)
    return (y_prompt.reshape(n_cseq, seq_c, d), y_sample.reshape(n_sseq, seq_s, d),
            new_k.reshape(kv_shape), new_v.reshape(kv_shape))
```

```python
import functools

import jax
import jax.numpy as jnp
from jax import lax
from jax.experimental import pallas as pl
from jax.experimental.pallas import tpu as pltpu

F32 = jnp.float32
BF16 = jnp.bfloat16

HEAD_DIM = 128
N_Q_HEADS = 8
N_KV_HEADS = 2
Q_PER_KV = N_Q_HEADS // N_KV_HEADS
ATTN_W = N_Q_HEADS * HEAD_DIM
KV_W = N_KV_HEADS * HEAD_DIM
GRID_W = 64
CONV_KERNEL = 31
CONV_PAD = CONV_KERNEL // 2
N_MOD = 6
ROPE_THETA = 10000.0
EPS = 1e-6

LANES = 128
CONV_HALO = 16
CONV_TILE = 256
CONV_CHUNK = 32
MIB = 1024 * 1024
RESIDENT = pl.Buffered(1)


def _params(semantics, vmem_mib):
    return pltpu.CompilerParams(dimension_semantics=semantics, vmem_limit_bytes=vmem_mib * MIB)


def _mod_row_fn(n_ctx, seq_s, tm):
    n_ctx_tiles = n_ctx // tm
    tiles_per_seq = seq_s // tm
    assert n_ctx % tm == 0 and seq_s % tm == 0
    return lambda i: jnp.where(i < n_ctx_tiles, 0, 1 + (i - n_ctx_tiles) // tiles_per_seq)


def _mod_kernel(c_ref, w_ref, b_ref, o_ref):
    a = jax.nn.silu(c_ref[...]).astype(BF16)
    o_ref[...] = jnp.dot(a, w_ref[...].astype(BF16), preferred_element_type=F32) + b_ref[...]


def _modulation(cond8, w_mod, b_mod, tn=1024):
    depth, d, n = w_mod.shape
    return pl.pallas_call(
        _mod_kernel,
        grid=(depth, n // tn),
        in_specs=[
            pl.BlockSpec((8, d), lambda l, j: (0, 0)),
            pl.BlockSpec((None, d, tn), lambda l, j: (l, 0, j)),
            pl.BlockSpec((None, 1, tn), lambda l, j: (l, 0, j)),
        ],
        out_specs=pl.BlockSpec((None, 8, tn), lambda l, j: (l, 0, j)),
        out_shape=jax.ShapeDtypeStruct((depth, 8, n), F32),
        compiler_params=_params(("arbitrary", "arbitrary"), 40),
        name="modulation",
    )(cond8, w_mod, b_mod.reshape(depth, 1, n))


def _rms_heads(acc, g, n_heads, cos2, sin2):
    outs = []
    for h in range(n_heads):
        y = acc[:, h * HEAD_DIM:(h + 1) * HEAD_DIM]
        y = y * lax.rsqrt(jnp.mean(y * y, axis=-1, keepdims=True) + EPS) * g
        outs.append(y * cos2 + pltpu.roll(y, HEAD_DIM // 2, 1) * sin2)
    return outs


def _in_kernel(x_ref, mod_ref, g1_ref, w_ref, qg_ref, kg_ref, cos_ref, sin_ref, *rest, n_ctx_tiles, tn):
    q_ref, kv_ref, u_ref, nk_ref, nv_ref = rest[-5:]
    x = x_ref[...]
    y = x * lax.rsqrt(jnp.mean(x * x, axis=-1, keepdims=True) + EPS) * g1_ref[...]
    h = (y * (1.0 + mod_ref[1:2, :]) + mod_ref[0:1, :]).astype(BF16)
    cos2, sin2 = cos_ref[...], sin_ref[...]

    def proj(c0, width):
        return jnp.dot(h, w_ref[:, c0:c0 + width], preferred_element_type=F32)

    for c0 in range(0, ATTN_W, tn):
        ys = _rms_heads(proj(c0, tn), qg_ref[...], tn // HEAD_DIM, cos2, sin2)
        for hh, yv in enumerate(ys):
            q_ref[:, c0 + hh * HEAD_DIM:c0 + (hh + 1) * HEAD_DIM] = yv.astype(BF16)

    acc = proj(ATTN_W, 2 * KV_W)
    for hh, yv in enumerate(_rms_heads(acc[:, :KV_W], kg_ref[...], N_KV_HEADS, cos2, sin2)):
        kv_ref[:, hh * HEAD_DIM:(hh + 1) * HEAD_DIM] = yv
    kv_ref[:, KV_W:] = acc[:, KV_W:]

    u0 = ATTN_W + 2 * KV_W
    for c0 in range(0, u_ref.shape[1], tn):
        u_ref[:, c0:c0 + tn] = proj(u0 + c0, tn)

    @pl.when(pl.program_id(0) < n_ctx_tiles)
    def _():
        seq = nk_ref.shape[1]
        for s in range(nk_ref.shape[0]):
            nk_ref[s] = kv_ref[s * seq:(s + 1) * seq, :KV_W]
            nv_ref[s] = kv_ref[s * seq:(s + 1) * seq, KV_W:]


def _in_projection(x, mod, g1, w_in, qg, kg, cos_ext, sin_ext, new_kv, layer, *, n_ctx, seq_c, seq_s,
                   tm=512, tn=512):
    n_tok, d = x.shape
    depth, _, n_cols = w_in.shape
    n_cseq = n_ctx // seq_c
    n_ctx_tiles = n_ctx // tm
    tiles_per_seq = seq_s // tm
    mod_row = _mod_row_fn(n_ctx, seq_s, tm)
    assert tm % seq_c == 0
    nkv_shape = (n_cseq, depth, seq_c, KV_W)
    nkv_spec = pl.BlockSpec((tm // seq_c, None, seq_c, KV_W),
                            lambda i: (jnp.minimum(i, n_ctx_tiles - 1), layer, 0, 0))
    rot_spec = pl.BlockSpec((tm, HEAD_DIM),
                            lambda i: (jnp.where(i < n_ctx_tiles, tiles_per_seq, i % tiles_per_seq), 0))
    head_spec = pl.BlockSpec((None, 1, HEAD_DIM), lambda i: (layer, 0, 0))
    in_specs = [
        pl.BlockSpec((tm, d), lambda i: (i, 0)),
        pl.BlockSpec((None, None, N_MOD, d), lambda i: (layer, mod_row(i), 0, 0)),
        pl.BlockSpec((None, 1, d), lambda i: (layer, 0, 0)),
        pl.BlockSpec((None, d, n_cols), lambda i: (layer, 0, 0), pipeline_mode=RESIDENT),
        head_spec, head_spec, rot_spec, rot_spec,
        pl.BlockSpec(memory_space=pl.ANY),
        pl.BlockSpec(memory_space=pl.ANY),
    ]
    args = [x, mod, g1, w_in, qg, kg, cos_ext, sin_ext, *new_kv]
    aliases = {len(args) - 2: 3, len(args) - 1: 4}
    return pl.pallas_call(
        functools.partial(_in_kernel, n_ctx_tiles=n_ctx_tiles, tn=tn),
        grid=(n_tok // tm,),
        in_specs=in_specs,
        out_specs=[
            pl.BlockSpec((tm, ATTN_W), lambda i: (i, 0)),
            pl.BlockSpec((tm, 2 * KV_W), lambda i: (i, 0)),
            pl.BlockSpec((tm, n_cols - ATTN_W - 2 * KV_W), lambda i: (i, 0)),
            nkv_spec,
            nkv_spec,
        ],
        out_shape=[
            jax.ShapeDtypeStruct((n_tok, ATTN_W), BF16),
            jax.ShapeDtypeStruct((n_tok, 2 * KV_W), F32),
            jax.ShapeDtypeStruct((n_tok, n_cols - ATTN_W - 2 * KV_W), F32),
            jax.ShapeDtypeStruct(nkv_shape, F32),
            jax.ShapeDtypeStruct(nkv_shape, F32),
        ],
        input_output_aliases=aliases,
        compiler_params=_params(("arbitrary",), 56),
        name="in_projection",
    )(*args)


def _softmax_pv(q, keys, vals, scale):
    dn = (((1,), (1,)), ((), ()))
    ss = [lax.dot_general(q, k, dn, preferred_element_type=F32) for k in keys]
    m = ss[0].max(axis=-1, keepdims=True)
    for s in ss[1:]:
        m = jnp.maximum(m, s.max(axis=-1, keepdims=True))
    es = [jnp.exp((s - m) * scale) for s in ss]
    den = es[0].sum(axis=-1, keepdims=True)
    for e in es[1:]:
        den = den + e.sum(axis=-1, keepdims=True)
    o = jnp.dot(es[0].astype(BF16), vals[0], preferred_element_type=F32)
    for e, v in zip(es[1:], vals[1:]):
        o = o + jnp.dot(e.astype(BF16), v, preferred_element_type=F32)
    return o / den


def _attn_ctx_kernel(q_ref, k_ref, v_ref, o_ref):
    scale = HEAD_DIM ** -0.5
    k = k_ref[...].astype(BF16)
    v = v_ref[...].astype(BF16)
    for g in range(Q_PER_KV):
        sl = slice(g * HEAD_DIM, (g + 1) * HEAD_DIM)
        o_ref[:, sl] = _softmax_pv(q_ref[:, sl], [k], [v], scale).astype(BF16)


def _attn_sample_kernel(q_ref, k_ref, v_ref, ck_ref, cv_ref, o_in, o_ref):
    del o_in
    scale = HEAD_DIM ** -0.5
    keys = [k_ref[...].astype(BF16), ck_ref[...].astype(BF16)]
    vals = [v_ref[...].astype(BF16), cv_ref[...].astype(BF16)]
    for g in range(Q_PER_KV):
        sl = slice(g * HEAD_DIM, (g + 1) * HEAD_DIM)
        o_ref[:, sl] = _softmax_pv(q_ref[:, sl], keys, vals, scale).astype(BF16)


def _attention_ctx(q, kv, *, n_seq, seq):
    gw = Q_PER_KV * HEAD_DIM
    return pl.pallas_call(
        _attn_ctx_kernel,
        grid=(n_seq, N_KV_HEADS),
        in_specs=[
            pl.BlockSpec((seq, gw), lambda b, h: (b, h)),
            pl.BlockSpec((seq, HEAD_DIM), lambda b, h: (b, h)),
            pl.BlockSpec((seq, HEAD_DIM), lambda b, h: (b, N_KV_HEADS + h)),
        ],
        out_specs=pl.BlockSpec((seq, gw), lambda b, h: (b, h)),
        out_shape=jax.ShapeDtypeStruct((q.shape[0], ATTN_W), BF16),
        compiler_params=_params(("arbitrary", "arbitrary"), 32),
        name="attention_ctx",
    )(q, kv, kv)


def _attention_sample(q, kv, cache_k, cache_v, attn, layer, *, n_ctx, n_seq, seq, tq=256):
    gw = Q_PER_KV * HEAD_DIM
    past = cache_k.shape[2]
    q0 = n_ctx // tq
    s0 = n_ctx // seq
    n_qt = seq // tq
    return pl.pallas_call(
        _attn_sample_kernel,
        grid=(n_seq, N_KV_HEADS, n_qt),
        in_specs=[
            pl.BlockSpec((tq, gw), lambda b, h, t: (q0 + b * n_qt + t, h)),
            pl.BlockSpec((seq, HEAD_DIM), lambda b, h, t: (s0 + b, h)),
            pl.BlockSpec((seq, HEAD_DIM), lambda b, h, t: (s0 + b, N_KV_HEADS + h)),
            pl.BlockSpec((None, None, past, HEAD_DIM), lambda b, h, t: (b, layer, 0, h)),
            pl.BlockSpec((None, None, past, HEAD_DIM), lambda b, h, t: (b, layer, 0, h)),
            pl.BlockSpec(memory_space=pl.ANY),
        ],
        out_specs=pl.BlockSpec((tq, gw), lambda b, h, t: (q0 + b * n_qt + t, h)),
        out_shape=jax.ShapeDtypeStruct(attn.shape, BF16),
        input_output_aliases={5: 0},
        compiler_params=_params(("arbitrary", "arbitrary", "arbitrary"), 32),
        name="attention_sample",
    )(q, kv, kv, cache_k, cache_v, attn)


def _conv_kernel(u1_ref, u2_ref, p1_ref, p2_ref, n1_ref, n2_ref, w_ref, b_ref, g_ref, beta_ref,
                 o_ref, pad_ref, cv_ref, *, n_ctx_tiles, tiles_per_seq):
    i = pl.program_id(0)
    n_lt = pad_ref.shape[0]
    t = CONV_TILE
    pos = (i - n_ctx_tiles) % tiles_per_seq
    is_smp = i >= n_ctx_tiles
    has_prev = jnp.logical_and(is_smp, pos != 0)
    has_next = jnp.logical_and(is_smp, pos != tiles_per_seq - 1)

    for c in range(n_lt):
        sl = slice(c * LANES, (c + 1) * LANES)
        prev = p1_ref[:, sl] * jax.nn.sigmoid(p2_ref[:, sl])
        nxt = n1_ref[:, sl] * jax.nn.sigmoid(n2_ref[:, sl])
        pad_ref[c, 0:CONV_HALO, :] = jnp.where(has_prev, prev, 0.0)
        pad_ref[c, CONV_HALO:CONV_HALO + t, :] = u1_ref[:, sl] * jax.nn.sigmoid(u2_ref[:, sl])
        pad_ref[c, CONV_HALO + t:, :] = jnp.where(has_next, nxt, 0.0)

    def lane_tile(c, carry):
        for r0 in range(0, t, CONV_CHUNK):
            acc = jnp.zeros((CONV_CHUNK, LANES), F32)
            for k in range(CONV_KERNEL):
                start = r0 + CONV_HALO - CONV_PAD + k
                acc = acc + pad_ref[c, start:start + CONV_CHUNK, :] * w_ref[c, k:k + 1, :]
            cv_ref[c, r0:r0 + CONV_CHUNK, :] = acc + b_ref[c]
        return carry

    lax.fori_loop(0, n_lt, lane_tile, 0)

    width = n_lt * LANES
    tot = cv_ref[0]
    for c in range(1, n_lt):
        tot = tot + cv_ref[c]
    mu = tot.sum(axis=-1, keepdims=True) / width
    sq = jnp.zeros_like(tot)
    for c in range(n_lt):
        d = cv_ref[c] - mu
        sq = sq + d * d
    rstd = lax.rsqrt(sq.sum(axis=-1, keepdims=True) / width + EPS)
    for c in range(n_lt):
        z = (cv_ref[c] - mu) * rstd * g_ref[c] + beta_ref[c]
        o_ref[:, c * LANES:(c + 1) * LANES] = (z * jax.nn.sigmoid(z)).astype(BF16)


def _lane_major(a):
    depth, rows, width = a.shape
    return a.reshape(depth, rows, width // LANES, LANES).transpose(0, 2, 1, 3)


def _conv_group(u, conv_w, conv_b, ln_g, ln_b, layer, *, n_ctx, seq_c, seq_s):
    n_tok = u.shape[0]
    width = u.shape[1] // 2
    n_lt = width // LANES
    t = CONV_TILE
    assert seq_c == t and seq_s % t == 0 and n_ctx % t == 0
    n_ctx_tiles = n_ctx // t
    tiles_per_seq = seq_s // t
    hb = t // CONV_HALO
    last_hb = n_tok // CONV_HALO - 1
    kern = functools.partial(_conv_kernel, n_ctx_tiles=n_ctx_tiles, tiles_per_seq=tiles_per_seq)
    prev_map = lambda c: (lambda i: (jnp.maximum(i * hb - 1, 0), c))
    next_map = lambda c: (lambda i: (jnp.minimum((i + 1) * hb, last_hb), c))
    vec_spec = pl.BlockSpec((None, n_lt, 1, LANES), lambda i: (layer, 0, 0, 0))
    return pl.pallas_call(
        kern,
        grid=(n_tok // t,),
        in_specs=[
            pl.BlockSpec((t, width), lambda i: (i, 0)),
            pl.BlockSpec((t, width), lambda i: (i, 1)),
            pl.BlockSpec((CONV_HALO, width), prev_map(0)),
            pl.BlockSpec((CONV_HALO, width), prev_map(1)),
            pl.BlockSpec((CONV_HALO, width), next_map(0)),
            pl.BlockSpec((CONV_HALO, width), next_map(1)),
            pl.BlockSpec((None, n_lt, CONV_KERNEL, LANES), lambda i: (layer, 0, 0, 0)),
            vec_spec, vec_spec, vec_spec,
        ],
        out_specs=pl.BlockSpec((t, width), lambda i: (i, 0)),
        out_shape=jax.ShapeDtypeStruct((n_tok, width), BF16),
        scratch_shapes=[
            pltpu.VMEM((n_lt, t + 2 * CONV_HALO, LANES), F32),
            pltpu.VMEM((n_lt, t, LANES), F32),
        ],
        compiler_params=_params(("arbitrary",), 32),
        name="conv_group",
    )(u, u, u, u, u, u, conv_w, conv_b, ln_g, ln_b)


def _out_kernel(a_ref, c_ref, x_ref, mod_ref, g2_ref, w_ref, xo_ref, h_ref):
    ka = a_ref.shape[1]
    mix = jnp.dot(a_ref[...], w_ref[0:ka, :], preferred_element_type=F32)
    mix = mix + jnp.dot(c_ref[...], w_ref[ka:, :], preferred_element_type=F32)
    x = x_ref[...] + mod_ref[2:3, :] * mix
    xo_ref[...] = x
    y = x * lax.rsqrt(jnp.mean(x * x, axis=-1, keepdims=True) + EPS) * g2_ref[...]
    h_ref[...] = (y * (1.0 + mod_ref[4:5, :]) + mod_ref[3:4, :]).astype(BF16)


def _out_projection(attn, cv, x, mod, g2, w_out, layer, *, n_ctx, seq_s, tm=512):
    n_tok, d = x.shape
    ka, kc = attn.shape[1], cv.shape[1]
    mod_row = _mod_row_fn(n_ctx, seq_s, tm)
    return pl.pallas_call(
        _out_kernel,
        grid=(n_tok // tm,),
        in_specs=[
            pl.BlockSpec((tm, ka), lambda i: (i, 0)),
            pl.BlockSpec((tm, kc), lambda i: (i, 0)),
            pl.BlockSpec((tm, d), lambda i: (i, 0)),
            pl.BlockSpec((None, None, N_MOD, d), lambda i: (layer, mod_row(i), 0, 0)),
            pl.BlockSpec((None, 1, d), lambda i: (layer, 0, 0)),
            pl.BlockSpec((None, ka + kc, d), lambda i: (layer, 0, 0), pipeline_mode=RESIDENT),
        ],
        out_specs=[
            pl.BlockSpec((tm, d), lambda i: (i, 0)),
            pl.BlockSpec((tm, d), lambda i: (i, 0)),
        ],
        out_shape=[
            jax.ShapeDtypeStruct((n_tok, d), F32),
            jax.ShapeDtypeStruct((n_tok, d), BF16),
        ],
        compiler_params=_params(("arbitrary",), 48),
        name="out_projection",
    )(attn, cv, x, mod, g2, w_out)


def _ffn_kernel(h_ref, x_ref, mod_ref, w1_ref, w2_ref, o_ref, *, tn):
    k = pl.program_id(1)
    d = o_ref.shape[1]

    @pl.when(k == 0)
    def _():
        o_ref[...] = jnp.zeros_like(o_ref)

    a = jnp.dot(h_ref[...], w1_ref[...], preferred_element_type=F32)
    a = jnp.square(jnp.maximum(a, 0.0)).astype(BF16)
    for n0 in range(0, d, tn):
        o_ref[:, n0:n0 + tn] += jnp.dot(a, w2_ref[:, n0:n0 + tn], preferred_element_type=F32)

    @pl.when(k == pl.num_programs(1) - 1)
    def _():
        o_ref[...] = x_ref[...] + mod_ref[5:6, :] * o_ref[...]


def _ffn(h, x, mod, w1, w2, layer, *, n_ctx, seq_s, tm=1024, tf=512, tn=512):
    n_tok, d = x.shape
    d_ff = w1.shape[2]
    mod_row = _mod_row_fn(n_ctx, seq_s, tm)
    return pl.pallas_call(
        functools.partial(_ffn_kernel, tn=tn),
        grid=(n_tok // tm, d_ff // tf),
        in_specs=[
            pl.BlockSpec((tm, d), lambda i, k: (i, 0)),
            pl.BlockSpec((tm, d), lambda i, k: (i, 0)),
            pl.BlockSpec((None, None, N_MOD, d), lambda i, k: (layer, mod_row(i), 0, 0)),
            pl.BlockSpec((None, d, tf), lambda i, k: (layer, 0, k)),
            pl.BlockSpec((None, tf, d), lambda i, k: (layer, k, 0)),
        ],
        out_specs=pl.BlockSpec((tm, d), lambda i, k: (i, 0)),
        out_shape=jax.ShapeDtypeStruct((n_tok, d), F32),
        compiler_params=_params(("arbitrary", "arbitrary"), 58),
        name="ffn",
    )(h, x, mod, w1, w2)


def _rotary_tables(n_tokens, n_identity):
    n_rows = n_tokens // GRID_W
    row = jnp.repeat(jnp.arange(n_rows, dtype=jnp.int32), GRID_W).astype(F32)
    col = jnp.tile(jnp.arange(GRID_W, dtype=jnp.int32), n_rows).astype(F32)
    n_pairs_axis = HEAD_DIM // 4
    freqs = ROPE_THETA ** (-jnp.arange(n_pairs_axis, dtype=F32) / n_pairs_axis)
    ang = jnp.concatenate([row[:, None] * freqs, col[:, None] * freqs], axis=-1)
    cos, sin = jnp.cos(ang), jnp.sin(ang)
    cos2 = jnp.concatenate([cos, cos], axis=-1)
    sin2 = jnp.concatenate([-sin, sin], axis=-1)
    return (jnp.concatenate([cos2, jnp.ones((n_identity, HEAD_DIM), F32)], axis=0),
            jnp.concatenate([sin2, jnp.zeros((n_identity, HEAD_DIM), F32)], axis=0))


def kernel(x_prompt, x_sample, cache_k, cache_v, c, c_ctx, norm1_g, w_mod, b_mod, w_in, q_norm_g, k_norm_g,
           conv_w, conv_b, conv_norm_g, conv_norm_b, w_out, norm2_g, w_ff1, w_ff2):
    n_cseq, seq_c, d = x_prompt.shape
    n_sseq, seq_s, _ = x_sample.shape
    depth = w_in.shape[0]
    n_ctx = n_cseq * seq_c
    n_smp = n_sseq * seq_s
    in_tm = 512

    x = jnp.concatenate([x_prompt.reshape(n_ctx, d), x_sample.reshape(n_smp, d)], axis=0)
    cond8 = jnp.concatenate([c_ctx[None, :], c, jnp.zeros((8 - 1 - n_sseq, d), F32)], axis=0)
    mod = _modulation(cond8, w_mod, b_mod).reshape(depth, 8, N_MOD, d)
    cos_ext, sin_ext = _rotary_tables(seq_s, in_tm)
    ck = cache_k.reshape(n_sseq, depth, cache_k.shape[2], KV_W)
    cvv = cache_v.reshape(n_sseq, depth, cache_v.shape[2], KV_W)

    g1, g2 = norm1_g[:, None, :], norm2_g[:, None, :]
    qg, kg = q_norm_g[:, None, :], k_norm_g[:, None, :]
    conv_w_lm = _lane_major(conv_w)
    conv_b_lm, ln_g_lm, ln_b_lm = (_lane_major(a[:, None, :]) for a in (conv_b, conv_norm_g, conv_norm_b))
    w_in_b, w_out_b, w_ff1_b, w_ff2_b = (w.astype(BF16) for w in (w_in, w_out, w_ff1, w_ff2))
    dims = dict(n_ctx=n_ctx, seq_s=seq_s)

    new_kv = [jnp.zeros((n_cseq, depth, seq_c, KV_W), F32) for _ in range(2)]
    for l in range(depth):
        q, kv, u, *new_kv = _in_projection(x, mod, g1, w_in_b, qg, kg, cos_ext, sin_ext, new_kv, l,
                                           seq_c=seq_c, tm=in_tm, **dims)
        attn = _attention_ctx(q, kv, n_seq=n_cseq, seq=seq_c)
        attn = _attention_sample(q, kv, ck, cvv, attn, l, n_ctx=n_ctx, n_seq=n_sseq, seq=seq_s)
        cv = _conv_group(u, conv_w_lm, conv_b_lm, ln_g_lm, ln_b_lm, l, n_ctx=n_ctx, seq_c=seq_c, seq_s=seq_s)
        x, h2 = _out_projection(attn, cv, x, mod, g2, w_out_b, l, **dims)
        x = _ffn(h2, x, mod, w_ff1_b, w_ff2_b, l, **dims)

    new_k, new_v = new_kv
    kv_shape = (n_cseq, depth, seq_c, N_KV_HEADS, HEAD_DIM)
    return (x[:n_ctx].reshape(n_cseq, seq_c, d), x[n_ctx:].reshape(n_sseq, seq_s, d),
            new_k.reshape(kv_shape), new_v.reshape(kv_shape))
```

```python
import functools

import jax
import jax.numpy as jnp
from jax import lax
from jax.experimental import pallas as pl
from jax.experimental.pallas import tpu as pltpu

F32 = jnp.float32
BF16 = jnp.bfloat16

HEAD_DIM = 128
N_Q_HEADS = 8
N_KV_HEADS = 2
Q_PER_KV = N_Q_HEADS // N_KV_HEADS
ATTN_W = N_Q_HEADS * HEAD_DIM
KV_W = N_KV_HEADS * HEAD_DIM
GRID_W = 64
CONV_KERNEL = 31
CONV_PAD = CONV_KERNEL // 2
N_MOD = 6
ROPE_THETA = 10000.0
EPS = 1e-6
LOG2E = 1.4426950408889634

LANES = 128
CONV_HALO = 16
CONV_TILE = 256
CONV_CHUNK = 32
MIB = 1024 * 1024
RESIDENT = pl.Buffered(1)


def _params(semantics, vmem_mib):
    return pltpu.CompilerParams(dimension_semantics=semantics, vmem_limit_bytes=vmem_mib * MIB)


def _mod_row_fn(n_ctx, seq_s, tm):
    n_ctx_tiles = n_ctx // tm
    tiles_per_seq = seq_s // tm
    assert n_ctx % tm == 0 and seq_s % tm == 0
    return lambda i: jnp.where(i < n_ctx_tiles, 0, 1 + (i - n_ctx_tiles) // tiles_per_seq)


def _mod_kernel(c_ref, w_ref, b_ref, o_ref):
    a = jax.nn.silu(c_ref[...]).astype(BF16)
    o_ref[...] = jnp.dot(a, w_ref[...].astype(BF16), preferred_element_type=F32) + b_ref[...]


def _modulation(cond8, w_mod, b_mod, tn=1024):
    depth, d, n = w_mod.shape
    return pl.pallas_call(
        _mod_kernel,
        grid=(depth, n // tn),
        in_specs=[
            pl.BlockSpec((8, d), lambda l, j: (0, 0)),
            pl.BlockSpec((None, d, tn), lambda l, j: (l, 0, j)),
            pl.BlockSpec((None, 1, tn), lambda l, j: (l, 0, j)),
        ],
        out_specs=pl.BlockSpec((None, 8, tn), lambda l, j: (l, 0, j)),
        out_shape=jax.ShapeDtypeStruct((depth, 8, n), F32),
        compiler_params=_params(("arbitrary", "arbitrary"), 40),
        name="modulation",
    )(cond8, w_mod, b_mod.reshape(depth, 1, n))


def _rms_heads(acc, g, n_heads, cos2, sin2):
    outs = []
    for h in range(n_heads):
        y = acc[:, h * HEAD_DIM:(h + 1) * HEAD_DIM]
        y = y * lax.rsqrt(jnp.mean(y * y, axis=-1, keepdims=True) + EPS) * g
        outs.append(y * cos2 + pltpu.roll(y, HEAD_DIM // 2, 1) * sin2)
    return outs


def _in_kernel(x_ref, mod_ref, g1_ref, w_ref, qg_ref, kg_ref, cos_ref, sin_ref, *rest, n_ctx_tiles, tn):
    q_ref, kv_ref, u_ref, nk_ref, nv_ref = rest[-5:]
    x = x_ref[...]
    y = x * lax.rsqrt(jnp.mean(x * x, axis=-1, keepdims=True) + EPS) * g1_ref[...]
    h = (y * (1.0 + mod_ref[1:2, :]) + mod_ref[0:1, :]).astype(BF16)
    cos2, sin2 = cos_ref[...], sin_ref[...]

    def proj(c0, width):
        return jnp.dot(h, w_ref[:, c0:c0 + width], preferred_element_type=F32)

    for c0 in range(0, ATTN_W, tn):
        ys = _rms_heads(proj(c0, tn), qg_ref[...], tn // HEAD_DIM, cos2, sin2)
        for hh, yv in enumerate(ys):
            q_ref[:, c0 + hh * HEAD_DIM:c0 + (hh + 1) * HEAD_DIM] = yv.astype(BF16)

    acc = proj(ATTN_W, 2 * KV_W)
    for hh, yv in enumerate(_rms_heads(acc[:, :KV_W], kg_ref[...], N_KV_HEADS, cos2, sin2)):
        kv_ref[:, hh * HEAD_DIM:(hh + 1) * HEAD_DIM] = yv
    kv_ref[:, KV_W:] = acc[:, KV_W:]

    u0 = ATTN_W + 2 * KV_W
    for c0 in range(0, u_ref.shape[1], tn):
        u_ref[:, c0:c0 + tn] = proj(u0 + c0, tn)

    @pl.when(pl.program_id(0) < n_ctx_tiles)
    def _():
        seq = nk_ref.shape[1]
        for s in range(nk_ref.shape[0]):
            nk_ref[s] = kv_ref[s * seq:(s + 1) * seq, :KV_W]
            nv_ref[s] = kv_ref[s * seq:(s + 1) * seq, KV_W:]


def _in_projection(x, mod, g1, w_in, qg, kg, cos_ext, sin_ext, new_kv, layer, *, n_ctx, seq_c, seq_s,
                   tm=512, tn=512):
    n_tok, d = x.shape
    depth, _, n_cols = w_in.shape
    n_cseq = n_ctx // seq_c
    n_ctx_tiles = n_ctx // tm
    tiles_per_seq = seq_s // tm
    mod_row = _mod_row_fn(n_ctx, seq_s, tm)
    assert tm % seq_c == 0
    nkv_shape = (n_cseq, depth, seq_c, KV_W)
    nkv_spec = pl.BlockSpec((tm // seq_c, None, seq_c, KV_W),
                            lambda i: (jnp.minimum(i, n_ctx_tiles - 1), layer, 0, 0))
    rot_spec = pl.BlockSpec((tm, HEAD_DIM),
                            lambda i: (jnp.where(i < n_ctx_tiles, tiles_per_seq, i % tiles_per_seq), 0))
    head_spec = pl.BlockSpec((None, 1, HEAD_DIM), lambda i: (layer, 0, 0))
    in_specs = [
        pl.BlockSpec((tm, d), lambda i: (i, 0)),
        pl.BlockSpec((None, None, N_MOD, d), lambda i: (layer, mod_row(i), 0, 0)),
        pl.BlockSpec((None, 1, d), lambda i: (layer, 0, 0)),
        pl.BlockSpec((None, d, n_cols), lambda i: (layer, 0, 0), pipeline_mode=RESIDENT),
        head_spec, head_spec, rot_spec, rot_spec,
        pl.BlockSpec(memory_space=pl.ANY),
        pl.BlockSpec(memory_space=pl.ANY),
    ]
    args = [x, mod, g1, w_in, qg, kg, cos_ext, sin_ext, *new_kv]
    aliases = {len(args) - 2: 3, len(args) - 1: 4}
    return pl.pallas_call(
        functools.partial(_in_kernel, n_ctx_tiles=n_ctx_tiles, tn=tn),
        grid=(n_tok // tm,),
        in_specs=in_specs,
        out_specs=[
            pl.BlockSpec((tm, ATTN_W), lambda i: (i, 0)),
            pl.BlockSpec((tm, 2 * KV_W), lambda i: (i, 0)),
            pl.BlockSpec((tm, n_cols - ATTN_W - 2 * KV_W), lambda i: (i, 0)),
            nkv_spec,
            nkv_spec,
        ],
        out_shape=[
            jax.ShapeDtypeStruct((n_tok, ATTN_W), BF16),
            jax.ShapeDtypeStruct((n_tok, 2 * KV_W), F32),
            jax.ShapeDtypeStruct((n_tok, n_cols - ATTN_W - 2 * KV_W), F32),
            jax.ShapeDtypeStruct(nkv_shape, F32),
            jax.ShapeDtypeStruct(nkv_shape, F32),
        ],
        input_output_aliases=aliases,
        compiler_params=_params(("arbitrary",), 56),
        name="in_projection",
    )(*args)


def _softmax_pv(q, keys, vals, scale):
    dn = (((1,), (1,)), ((), ()))
    ss = [lax.dot_general(q, k, dn, preferred_element_type=F32) for k in keys]
    m = ss[0].max(axis=-1, keepdims=True)
    for s in ss[1:]:
        m = jnp.maximum(m, s.max(axis=-1, keepdims=True))
    es = [jnp.exp2((s - m) * (scale * LOG2E)) for s in ss]
    den = es[0].sum(axis=-1, keepdims=True)
    for e in es[1:]:
        den = den + e.sum(axis=-1, keepdims=True)
    o = jnp.dot(es[0].astype(BF16), vals[0], preferred_element_type=F32)
    for e, v in zip(es[1:], vals[1:]):
        o = o + jnp.dot(e.astype(BF16), v, preferred_element_type=F32)
    return o / den


def _attn_kernel(q_ref, kc_ref, vc_ref, ks_ref, vs_ref, ck_ref, cv_ref, o_ref, *, n_ctx_steps):
    scale = HEAD_DIM ** -0.5
    is_ctx = pl.program_id(0) < n_ctx_steps

    def attend(key_refs, val_refs):
        keys = [r[...].astype(BF16) for r in key_refs]
        vals = [r[...].astype(BF16) for r in val_refs]
        for g in range(Q_PER_KV):
            sl = slice(g * HEAD_DIM, (g + 1) * HEAD_DIM)
            o_ref[:, sl] = _softmax_pv(q_ref[:, sl], keys, vals, scale).astype(BF16)

    pl.when(is_ctx)(lambda: attend([kc_ref], [vc_ref]))
    pl.when(jnp.logical_not(is_ctx))(lambda: attend([ks_ref, ck_ref], [vs_ref, cv_ref]))


def _attention(q, kv, cache_k, cache_v, layer, *, n_ctx, seq_c, seq_s):
    gw = Q_PER_KV * HEAD_DIM
    tq = seq_c
    n_tok = q.shape[0]
    n_cseq = n_ctx // seq_c
    n_sseq = (n_tok - n_ctx) // seq_s
    past = cache_k.shape[2]
    n_qt = seq_s // tq
    n_ctx_steps = n_cseq * N_KV_HEADS
    n_smp_steps = n_sseq * N_KV_HEADS * n_qt
    q0 = n_ctx // tq
    s0 = n_ctx // seq_s
    assert n_ctx % seq_s == 0

    def ctx_bh(s):
        s = jnp.minimum(s, n_ctx_steps - 1)
        return s // N_KV_HEADS, s % N_KV_HEADS

    def smp_bht(s):
        r = jnp.maximum(s - n_ctx_steps, 0)
        return r // (N_KV_HEADS * n_qt), (r // n_qt) % N_KV_HEADS, r % n_qt

    def q_map(s):
        cb, ch = ctx_bh(s)
        sb, sh, st = smp_bht(s)
        is_ctx = s < n_ctx_steps
        return jnp.where(is_ctx, cb, q0 + sb * n_qt + st), jnp.where(is_ctx, ch, sh)

    return pl.pallas_call(
        functools.partial(_attn_kernel, n_ctx_steps=n_ctx_steps),
        grid=(n_ctx_steps + n_smp_steps,),
        in_specs=[
            pl.BlockSpec((tq, gw), q_map),
            pl.BlockSpec((seq_c, HEAD_DIM), lambda s: ctx_bh(s)),
            pl.BlockSpec((seq_c, HEAD_DIM), lambda s: (ctx_bh(s)[0], N_KV_HEADS + ctx_bh(s)[1])),
            pl.BlockSpec((seq_s, HEAD_DIM), lambda s: (s0 + smp_bht(s)[0], smp_bht(s)[1])),
            pl.BlockSpec((seq_s, HEAD_DIM), lambda s: (s0 + smp_bht(s)[0], N_KV_HEADS + smp_bht(s)[1])),
            pl.BlockSpec((None, None, past, HEAD_DIM), lambda s: (smp_bht(s)[0], layer, 0, smp_bht(s)[1])),
            pl.BlockSpec((None, None, past, HEAD_DIM), lambda s: (smp_bht(s)[0], layer, 0, smp_bht(s)[1])),
        ],
        out_specs=pl.BlockSpec((tq, gw), q_map),
        out_shape=jax.ShapeDtypeStruct((n_tok, ATTN_W), BF16),
        compiler_params=_params(("arbitrary",), 32),
        name="attention",
    )(q, kv, kv, kv, kv, cache_k, cache_v)


def _lane_major(a):
    depth, rows, width = a.shape
    return a.reshape(depth, rows, width // LANES, LANES).transpose(0, 2, 1, 3)


def _glu(a, b):
    return a * jax.nn.sigmoid(b)


def _conv_lane_tile(c, u1_ref, u2_ref, p1_ref, p2_ref, n1_ref, n2_ref, cw_ref, cb_ref, pad_ref, cvf_ref,
                    anchor, *, tile, n_ctx_tiles, tiles_per_seq):
    n_sub = pad_ref.shape[0]
    sub, halo = CONV_TILE, CONV_HALO
    is_smp = tile >= n_ctx_tiles
    pos = (tile - n_ctx_tiles) % tiles_per_seq
    has_prev = jnp.logical_and(is_smp, pos != 0)
    has_next = jnp.logical_and(is_smp, pos != tiles_per_seq - 1)

    sl = slice(c * LANES, (c + 1) * LANES)
    g = _glu(u1_ref[:, sl], u2_ref[:, sl])
    if anchor is not None:
        g = g + anchor
    for s in range(n_sub):
        if s == 0:
            lo = jnp.where(has_prev, _glu(p1_ref[:, sl], p2_ref[:, sl]), 0.0)
        else:
            lo = jnp.where(is_smp, g[s * sub - halo:s * sub], 0.0)
        if s == n_sub - 1:
            hi = jnp.where(has_next, _glu(n1_ref[:, sl], n2_ref[:, sl]), 0.0)
        else:
            hi = jnp.where(is_smp, g[(s + 1) * sub:(s + 1) * sub + halo], 0.0)
        pad_ref[s, c, 0:halo, :] = lo
        pad_ref[s, c, halo:halo + sub, :] = g[s * sub:(s + 1) * sub]
        pad_ref[s, c, halo + sub:, :] = hi

    for s in range(n_sub):
        for r0 in range(0, sub, CONV_CHUNK):
            acc = jnp.zeros((CONV_CHUNK, LANES), F32)
            for k in range(CONV_KERNEL):
                start = r0 + halo - CONV_PAD + k
                acc = acc + pad_ref[s, c, start:start + CONV_CHUNK, :] * cw_ref[c, k:k + 1, :]
            cvf_ref[c, s * sub + r0:s * sub + r0 + CONV_CHUNK, :] = acc + cb_ref[c]


def _conv_finish(cvf_ref, lg_ref, lb_ref, cv_ref):
    n_lt = cvf_ref.shape[0]
    width = n_lt * LANES
    tot = cvf_ref[0]
    for c in range(1, n_lt):
        tot = tot + cvf_ref[c]
    mu = tot.sum(axis=-1, keepdims=True) / width
    sq = jnp.zeros_like(tot)
    for c in range(n_lt):
        d = cvf_ref[c] - mu
        sq = sq + d * d
    rstd = lax.rsqrt(sq.sum(axis=-1, keepdims=True) / width + EPS)
    for c in range(n_lt):
        z = (cvf_ref[c] - mu) * rstd * lg_ref[c] + lb_ref[c]
        cv_ref[:, c * LANES:(c + 1) * LANES] = (z * jax.nn.sigmoid(z)).astype(BF16)


def _out_chunk(c, n_chunks, a_ref, cv_ref, x_ref, mod_ref, w_ref, xo_ref):
    ka = a_ref.shape[1]
    ncols = xo_ref.shape[1] // n_chunks
    cols = slice(c * ncols, (c + 1) * ncols)
    mix = jnp.dot(a_ref[...], w_ref[0:ka, cols], preferred_element_type=F32)
    mix = mix + jnp.dot(cv_ref[...], w_ref[ka:, cols], preferred_element_type=F32)
    xc = x_ref[:, cols] + mod_ref[2:3, cols] * mix
    xo_ref[:, cols] = xc
    return jnp.sum(xc * xc, axis=-1, keepdims=True), xc[0:1, 0:LANES]


def _zero_after(v):
    bits = lax.bitcast_convert_type(v, jnp.uint32)
    return lax.bitcast_convert_type((bits >> 16) >> 16, F32)


def _out_finish(ssq, xo_ref, mod_ref, g2_ref, h_ref):
    x = xo_ref[...]
    y = x * lax.rsqrt(ssq / x.shape[1] + EPS) * g2_ref[...]
    h_ref[...] = (y * (1.0 + mod_ref[4:5, :]) + mod_ref[3:4, :]).astype(BF16)


def _mix_kernel(u1_ref, u2_ref, p1_ref, p2_ref, n1_ref, n2_ref, cw_ref, cb_ref, lg_ref, lb_ref,
                a_ref, x_ref, mod_ref, g2_ref, w_ref, xo_ref, h_ref, pad_ref, cvf_ref, cur_ref, nxt_ref,
                *, n_ctx_tiles, tiles_per_seq):
    i = pl.program_id(0)
    n_tiles = pl.num_programs(0) - 1
    n_lt = cvf_ref.shape[0]

    def step(do_conv, do_out, cv_dst):
        ssq = None
        for c in range(n_lt):
            anchor = None
            if do_out:
                part, probe = _out_chunk(c, n_lt, a_ref, cur_ref, x_ref, mod_ref, w_ref, xo_ref)
                ssq = part if ssq is None else ssq + part
                anchor = _zero_after(probe)
            if do_conv:
                _conv_lane_tile(c, u1_ref, u2_ref, p1_ref, p2_ref, n1_ref, n2_ref, cw_ref, cb_ref, pad_ref,
                                cvf_ref, anchor, tile=i, n_ctx_tiles=n_ctx_tiles, tiles_per_seq=tiles_per_seq)
        if do_conv:
            _conv_finish(cvf_ref, lg_ref, lb_ref, cv_dst)
        if do_out:
            _out_finish(ssq, xo_ref, mod_ref, g2_ref, h_ref)

    @pl.when(i == 0)
    def _():
        step(True, False, cur_ref)

    @pl.when(jnp.logical_and(i > 0, i < n_tiles))
    def _():
        step(True, True, nxt_ref)
        cur_ref[...] = nxt_ref[...]

    @pl.when(i == n_tiles)
    def _():
        step(False, True, None)


def _mix_projection(u, attn, x, mod, g2, w_out, conv_w, conv_b, ln_g, ln_b, layer, *, n_ctx, seq_c, seq_s, tm=512):
    n_tok, d = x.shape
    width = u.shape[1] // 2
    n_lt = width // LANES
    ka = attn.shape[1]
    assert seq_c == CONV_TILE and tm % CONV_TILE == 0 and seq_s % tm == 0 and n_ctx % tm == 0
    n_tiles = n_tok // tm
    n_ctx_tiles = n_ctx // tm
    tiles_per_seq = seq_s // tm
    hb = tm // CONV_HALO
    last_hb = n_tok // CONV_HALO - 1
    mod_row = _mod_row_fn(n_ctx, seq_s, tm)
    conv_tile = lambda i: jnp.minimum(i, n_tiles - 1)
    out_tile = lambda i: jnp.maximum(i - 1, 0)
    prev_map = lambda c: (lambda i: (jnp.maximum(conv_tile(i) * hb - 1, 0), c))
    next_map = lambda c: (lambda i: (jnp.minimum((conv_tile(i) + 1) * hb, last_hb), c))
    vec_spec = pl.BlockSpec((None, n_lt, 1, LANES), lambda i: (layer, 0, 0, 0))
    kern = functools.partial(_mix_kernel, n_ctx_tiles=n_ctx_tiles, tiles_per_seq=tiles_per_seq)
    return pl.pallas_call(
        kern,
        grid=(n_tiles + 1,),
        in_specs=[
            pl.BlockSpec((tm, width), lambda i: (conv_tile(i), 0)),
            pl.BlockSpec((tm, width), lambda i: (conv_tile(i), 1)),
            pl.BlockSpec((CONV_HALO, width), prev_map(0)),
            pl.BlockSpec((CONV_HALO, width), prev_map(1)),
            pl.BlockSpec((CONV_HALO, width), next_map(0)),
            pl.BlockSpec((CONV_HALO, width), next_map(1)),
            pl.BlockSpec((None, n_lt, CONV_KERNEL, LANES), lambda i: (layer, 0, 0, 0)),
            vec_spec, vec_spec, vec_spec,
            pl.BlockSpec((tm, ka), lambda i: (out_tile(i), 0)),
            pl.BlockSpec((tm, d), lambda i: (out_tile(i), 0)),
            pl.BlockSpec((None, None, N_MOD, d), lambda i: (layer, mod_row(out_tile(i)), 0, 0)),
            pl.BlockSpec((None, 1, d), lambda i: (layer, 0, 0)),
            pl.BlockSpec((None, ka + width, d), lambda i: (layer, 0, 0), pipeline_mode=RESIDENT),
        ],
        out_specs=[
            pl.BlockSpec((tm, d), lambda i: (out_tile(i), 0)),
            pl.BlockSpec((tm, d), lambda i: (out_tile(i), 0)),
        ],
        out_shape=[
            jax.ShapeDtypeStruct((n_tok, d), F32),
            jax.ShapeDtypeStruct((n_tok, d), BF16),
        ],
        scratch_shapes=[
            pltpu.VMEM((tm // CONV_TILE, n_lt, CONV_TILE + 2 * CONV_HALO, LANES), F32),
            pltpu.VMEM((n_lt, tm, LANES), F32),
            pltpu.VMEM((tm, width), BF16),
            pltpu.VMEM((tm, width), BF16),
        ],
        compiler_params=_params(("arbitrary",), 56),
        name="mix_projection",
    )(u, u, u, u, u, u, conv_w, conv_b, ln_g, ln_b, attn, x, mod, g2, w_out)


def _ffn_kernel(h_ref, x_ref, mod_ref, w1_ref, w2_ref, o_ref, *, tn):
    k = pl.program_id(1)
    d = o_ref.shape[1]

    @pl.when(k == 0)
    def _():
        o_ref[...] = jnp.zeros_like(o_ref)

    a = jnp.dot(h_ref[...], w1_ref[...], preferred_element_type=F32)
    a = jnp.square(jnp.maximum(a, 0.0)).astype(BF16)
    for n0 in range(0, d, tn):
        o_ref[:, n0:n0 + tn] += jnp.dot(a, w2_ref[:, n0:n0 + tn], preferred_element_type=F32)

    @pl.when(k == pl.num_programs(1) - 1)
    def _():
        o_ref[...] = x_ref[...] + mod_ref[5:6, :] * o_ref[...]


def _ffn(h, x, mod, w1, w2, layer, *, n_ctx, seq_s, tm=1024, tf=512, tn=512):
    n_tok, d = x.shape
    d_ff = w1.shape[2]
    mod_row = _mod_row_fn(n_ctx, seq_s, tm)
    return pl.pallas_call(
        functools.partial(_ffn_kernel, tn=tn),
        grid=(n_tok // tm, d_ff // tf),
        in_specs=[
            pl.BlockSpec((tm, d), lambda i, k: (i, 0)),
            pl.BlockSpec((tm, d), lambda i, k: (i, 0)),
            pl.BlockSpec((None, None, N_MOD, d), lambda i, k: (layer, mod_row(i), 0, 0)),
            pl.BlockSpec((None, d, tf), lambda i, k: (layer, 0, k)),
            pl.BlockSpec((None, tf, d), lambda i, k: (layer, k, 0)),
        ],
        out_specs=pl.BlockSpec((tm, d), lambda i, k: (i, 0)),
        out_shape=jax.ShapeDtypeStruct((n_tok, d), F32),
        compiler_params=_params(("arbitrary", "arbitrary"), 58),
        name="ffn",
    )(h, x, mod, w1, w2)


def _rotary_tables(n_tokens, n_identity):
    n_rows = n_tokens // GRID_W
    row = jnp.repeat(jnp.arange(n_rows, dtype=jnp.int32), GRID_W).astype(F32)
    col = jnp.tile(jnp.arange(GRID_W, dtype=jnp.int32), n_rows).astype(F32)
    n_pairs_axis = HEAD_DIM // 4
    freqs = ROPE_THETA ** (-jnp.arange(n_pairs_axis, dtype=F32) / n_pairs_axis)
    ang = jnp.concatenate([row[:, None] * freqs, col[:, None] * freqs], axis=-1)
    cos, sin = jnp.cos(ang), jnp.sin(ang)
    cos2 = jnp.concatenate([cos, cos], axis=-1)
    sin2 = jnp.concatenate([-sin, sin], axis=-1)
    return (jnp.concatenate([cos2, jnp.ones((n_identity, HEAD_DIM), F32)], axis=0),
            jnp.concatenate([sin2, jnp.zeros((n_identity, HEAD_DIM), F32)], axis=0))


def kernel(x_prompt, x_sample, cache_k, cache_v, c, c_ctx, norm1_g, w_mod, b_mod, w_in, q_norm_g, k_norm_g,
           conv_w, conv_b, conv_norm_g, conv_norm_b, w_out, norm2_g, w_ff1, w_ff2):
    n_cseq, seq_c, d = x_prompt.shape
    n_sseq, seq_s, _ = x_sample.shape
    depth = w_in.shape[0]
    n_ctx = n_cseq * seq_c
    n_smp = n_sseq * seq_s
    in_tm = 512

    x = jnp.concatenate([x_prompt.reshape(n_ctx, d), x_sample.reshape(n_smp, d)], axis=0)
    cond8 = jnp.concatenate([c_ctx[None, :], c, jnp.zeros((8 - 1 - n_sseq, d), F32)], axis=0)
    mod = _modulation(cond8, w_mod, b_mod).reshape(depth, 8, N_MOD, d)
    cos_ext, sin_ext = _rotary_tables(seq_s, in_tm)
    ck = cache_k.reshape(n_sseq, depth, cache_k.shape[2], KV_W)
    cvv = cache_v.reshape(n_sseq, depth, cache_v.shape[2], KV_W)

    g1, g2 = norm1_g[:, None, :], norm2_g[:, None, :]
    qg, kg = q_norm_g[:, None, :], k_norm_g[:, None, :]
    conv_w_lm = _lane_major(conv_w)
    conv_b_lm, ln_g_lm, ln_b_lm = (_lane_major(a[:, None, :]) for a in (conv_b, conv_norm_g, conv_norm_b))
    w_in_b, w_out_b, w_ff1_b, w_ff2_b = (w.astype(BF16) for w in (w_in, w_out, w_ff1, w_ff2))
    dims = dict(n_ctx=n_ctx, seq_s=seq_s)

    new_kv = [jnp.zeros((n_cseq, depth, seq_c, KV_W), F32) for _ in range(2)]
    for l in range(depth):
        q, kv, u, *new_kv = _in_projection(x, mod, g1, w_in_b, qg, kg, cos_ext, sin_ext, new_kv, l,
                                           seq_c=seq_c, tm=in_tm, **dims)
        attn = _attention(q, kv, ck, cvv, l, n_ctx=n_ctx, seq_c=seq_c, seq_s=seq_s)
        x, h2 = _mix_projection(u, attn, x, mod, g2, w_out_b, conv_w_lm, conv_b_lm, ln_g_lm, ln_b_lm, l,
                                seq_c=seq_c, **dims)
        x = _ffn(h2, x, mod, w_ff1_b, w_ff2_b, l, **dims)

    new_k, new_v = new_kv
    kv_shape = (n_cseq, depth, seq_c, N_KV_HEADS, HEAD_DIM)
    return (x[:n_ctx].reshape(n_cseq, seq_c, d), x[n_ctx:].reshape(n_sseq, seq_s, d),
            new_k.reshape(kv_shape), new_v.reshape(kv_shape))
```

```python
import functools

import jax
import jax.numpy as jnp
from jax import lax
from jax.experimental import pallas as pl
from jax.experimental.pallas import tpu as pltpu

F32 = jnp.float32
BF16 = jnp.bfloat16

HEAD_DIM = 128
N_Q_HEADS = 8
N_KV_HEADS = 2
Q_PER_KV = N_Q_HEADS // N_KV_HEADS
ATTN_W = N_Q_HEADS * HEAD_DIM
KV_W = N_KV_HEADS * HEAD_DIM
GRID_W = 64
CONV_KERNEL = 31
CONV_PAD = CONV_KERNEL // 2
N_MOD = 6
ROPE_THETA = 10000.0
EPS = 1e-6
LOG2E = 1.4426950408889634

LANES = 128
CONV_HALO = 16
CONV_TILE = 256
CONV_CHUNK = 32
MIB = 1024 * 1024
RESIDENT = pl.Buffered(1)


def _params(semantics, vmem_mib):
    return pltpu.CompilerParams(dimension_semantics=semantics, vmem_limit_bytes=vmem_mib * MIB)


def _mod_row_fn(n_ctx, seq_s, tm):
    n_ctx_tiles = n_ctx // tm
    tiles_per_seq = seq_s // tm
    assert n_ctx % tm == 0 and seq_s % tm == 0
    return lambda i: jnp.where(i < n_ctx_tiles, 0, 1 + (i - n_ctx_tiles) // tiles_per_seq)


def _mod_kernel(c_ref, w_ref, b_ref, o_ref):
    a = jax.nn.silu(c_ref[...]).astype(BF16)
    o_ref[...] = jnp.dot(a, w_ref[...].astype(BF16), preferred_element_type=F32) + b_ref[...]


def _modulation(cond8, w_mod, b_mod, tn=1024):
    depth, d, n = w_mod.shape
    return pl.pallas_call(
        _mod_kernel,
        grid=(depth, n // tn),
        in_specs=[
            pl.BlockSpec((8, d), lambda l, j: (0, 0)),
            pl.BlockSpec((None, d, tn), lambda l, j: (l, 0, j)),
            pl.BlockSpec((None, 1, tn), lambda l, j: (l, 0, j)),
        ],
        out_specs=pl.BlockSpec((None, 8, tn), lambda l, j: (l, 0, j)),
        out_shape=jax.ShapeDtypeStruct((depth, 8, n), F32),
        compiler_params=_params(("arbitrary", "arbitrary"), 40),
        name="modulation",
    )(cond8, w_mod, b_mod.reshape(depth, 1, n))


def _rms_heads(acc, g, n_heads, cos2, sin2):
    outs = []
    for h in range(n_heads):
        y = acc[:, h * HEAD_DIM:(h + 1) * HEAD_DIM]
        y = y * lax.rsqrt(jnp.mean(y * y, axis=-1, keepdims=True) + EPS) * g
        outs.append(y * cos2 + pltpu.roll(y, HEAD_DIM // 2, 1) * sin2)
    return outs


def _in_kernel(x_ref, mod_ref, g1_ref, w_ref, qg_ref, kg_ref, cos_ref, sin_ref, *rest, n_ctx_tiles, tn):
    q_ref, kv_ref, u_ref, nk_ref, nv_ref = rest[-5:]
    x = x_ref[...]
    y = x * lax.rsqrt(jnp.mean(x * x, axis=-1, keepdims=True) + EPS) * g1_ref[...]
    h = (y * (1.0 + mod_ref[1:2, :]) + mod_ref[0:1, :]).astype(BF16)
    cos2, sin2 = cos_ref[...], sin_ref[...]

    def proj(c0, width):
        return jnp.dot(h, w_ref[:, c0:c0 + width], preferred_element_type=F32)

    for c0 in range(0, ATTN_W, tn):
        ys = _rms_heads(proj(c0, tn), qg_ref[...], tn // HEAD_DIM, cos2, sin2)
        for hh, yv in enumerate(ys):
            q_ref[:, c0 + hh * HEAD_DIM:c0 + (hh + 1) * HEAD_DIM] = yv.astype(BF16)

    acc = proj(ATTN_W, 2 * KV_W)
    for hh, yv in enumerate(_rms_heads(acc[:, :KV_W], kg_ref[...], N_KV_HEADS, cos2, sin2)):
        kv_ref[:, hh * HEAD_DIM:(hh + 1) * HEAD_DIM] = yv
    kv_ref[:, KV_W:] = acc[:, KV_W:]

    u0 = ATTN_W + 2 * KV_W
    for c0 in range(0, u_ref.shape[1], tn):
        u_ref[:, c0:c0 + tn] = proj(u0 + c0, tn)

    @pl.when(pl.program_id(0) < n_ctx_tiles)
    def _():
        seq = nk_ref.shape[1]
        for s in range(nk_ref.shape[0]):
            nk_ref[s] = kv_ref[s * seq:(s + 1) * seq, :KV_W]
            nv_ref[s] = kv_ref[s * seq:(s + 1) * seq, KV_W:]


def _in_projection(x, mod, g1, w_in, qg, kg, cos_ext, sin_ext, new_kv, layer, *, n_ctx, seq_c, seq_s,
                   tm=512, tn=512):
    n_tok, d = x.shape
    depth, _, n_cols = w_in.shape
    n_cseq = n_ctx // seq_c
    n_ctx_tiles = n_ctx // tm
    tiles_per_seq = seq_s // tm
    mod_row = _mod_row_fn(n_ctx, seq_s, tm)
    assert tm % seq_c == 0
    nkv_shape = (n_cseq, depth, seq_c, KV_W)
    nkv_spec = pl.BlockSpec((tm // seq_c, None, seq_c, KV_W),
                            lambda i: (jnp.minimum(i, n_ctx_tiles - 1), layer, 0, 0))
    rot_spec = pl.BlockSpec((tm, HEAD_DIM),
                            lambda i: (jnp.where(i < n_ctx_tiles, tiles_per_seq, i % tiles_per_seq), 0))
    head_spec = pl.BlockSpec((None, 1, HEAD_DIM), lambda i: (layer, 0, 0))
    in_specs = [
        pl.BlockSpec((tm, d), lambda i: (i, 0)),
        pl.BlockSpec((None, None, N_MOD, d), lambda i: (layer, mod_row(i), 0, 0)),
        pl.BlockSpec((None, 1, d), lambda i: (layer, 0, 0)),
        pl.BlockSpec((None, d, n_cols), lambda i: (layer, 0, 0), pipeline_mode=RESIDENT),
        head_spec, head_spec, rot_spec, rot_spec,
        pl.BlockSpec(memory_space=pl.ANY),
        pl.BlockSpec(memory_space=pl.ANY),
    ]
    args = [x, mod, g1, w_in, qg, kg, cos_ext, sin_ext, *new_kv]
    aliases = {len(args) - 2: 3, len(args) - 1: 4}
    return pl.pallas_call(
        functools.partial(_in_kernel, n_ctx_tiles=n_ctx_tiles, tn=tn),
        grid=(n_tok // tm,),
        in_specs=in_specs,
        out_specs=[
            pl.BlockSpec((tm, ATTN_W), lambda i: (i, 0)),
            pl.BlockSpec((tm, 2 * KV_W), lambda i: (i, 0)),
            pl.BlockSpec((tm, n_cols - ATTN_W - 2 * KV_W), lambda i: (i, 0)),
            nkv_spec,
            nkv_spec,
        ],
        out_shape=[
            jax.ShapeDtypeStruct((n_tok, ATTN_W), BF16),
            jax.ShapeDtypeStruct((n_tok, 2 * KV_W), F32),
            jax.ShapeDtypeStruct((n_tok, n_cols - ATTN_W - 2 * KV_W), F32),
            jax.ShapeDtypeStruct(nkv_shape, F32),
            jax.ShapeDtypeStruct(nkv_shape, F32),
        ],
        input_output_aliases=aliases,
        compiler_params=_params(("arbitrary",), 56),
        name="in_projection",
    )(*args)


def _softmax_pv(q, keys, vals, scale):
    dn = (((1,), (1,)), ((), ()))
    ss = [lax.dot_general(q, k, dn, preferred_element_type=F32) for k in keys]
    m = ss[0].max(axis=-1, keepdims=True)
    for s in ss[1:]:
        m = jnp.maximum(m, s.max(axis=-1, keepdims=True))
    es = [jnp.exp2((s - m) * (scale * LOG2E)) for s in ss]
    den = es[0].sum(axis=-1, keepdims=True)
    for e in es[1:]:
        den = den + e.sum(axis=-1, keepdims=True)
    o = jnp.dot(es[0].astype(BF16), vals[0], preferred_element_type=F32)
    for e, v in zip(es[1:], vals[1:]):
        o = o + jnp.dot(e.astype(BF16), v, preferred_element_type=F32)
    return o / den


def _attn_kernel(q_ref, kc_ref, vc_ref, ks_ref, vs_ref, ck_ref, cv_ref, o_ref, *, n_ctx_steps):
    scale = HEAD_DIM ** -0.5
    is_ctx = pl.program_id(0) < n_ctx_steps

    def attend(key_refs, val_refs):
        keys = [r[...].astype(BF16) for r in key_refs]
        vals = [r[...].astype(BF16) for r in val_refs]
        for g in range(Q_PER_KV):
            sl = slice(g * HEAD_DIM, (g + 1) * HEAD_DIM)
            o_ref[:, sl] = _softmax_pv(q_ref[:, sl], keys, vals, scale).astype(BF16)

    pl.when(is_ctx)(lambda: attend([kc_ref], [vc_ref]))
    pl.when(jnp.logical_not(is_ctx))(lambda: attend([ks_ref, ck_ref], [vs_ref, cv_ref]))


def _attention(q, kv, cache_k, cache_v, layer, *, n_ctx, seq_c, seq_s):
    gw = Q_PER_KV * HEAD_DIM
    tq = seq_c
    n_tok = q.shape[0]
    n_cseq = n_ctx // seq_c
    n_sseq = (n_tok - n_ctx) // seq_s
    past = cache_k.shape[2]
    n_qt = seq_s // tq
    n_ctx_steps = n_cseq * N_KV_HEADS
    n_smp_steps = n_sseq * N_KV_HEADS * n_qt
    q0 = n_ctx // tq
    s0 = n_ctx // seq_s
    assert n_ctx % seq_s == 0

    def ctx_bh(s):
        s = jnp.minimum(s, n_ctx_steps - 1)
        return s // N_KV_HEADS, s % N_KV_HEADS

    def smp_bht(s):
        r = jnp.maximum(s - n_ctx_steps, 0)
        return r // (N_KV_HEADS * n_qt), (r // n_qt) % N_KV_HEADS, r % n_qt

    def q_map(s):
        cb, ch = ctx_bh(s)
        sb, sh, st = smp_bht(s)
        is_ctx = s < n_ctx_steps
        return jnp.where(is_ctx, cb, q0 + sb * n_qt + st), jnp.where(is_ctx, ch, sh)

    return pl.pallas_call(
        functools.partial(_attn_kernel, n_ctx_steps=n_ctx_steps),
        grid=(n_ctx_steps + n_smp_steps,),
        in_specs=[
            pl.BlockSpec((tq, gw), q_map),
            pl.BlockSpec((seq_c, HEAD_DIM), lambda s: ctx_bh(s)),
            pl.BlockSpec((seq_c, HEAD_DIM), lambda s: (ctx_bh(s)[0], N_KV_HEADS + ctx_bh(s)[1])),
            pl.BlockSpec((seq_s, HEAD_DIM), lambda s: (s0 + smp_bht(s)[0], smp_bht(s)[1])),
            pl.BlockSpec((seq_s, HEAD_DIM), lambda s: (s0 + smp_bht(s)[0], N_KV_HEADS + smp_bht(s)[1])),
            pl.BlockSpec((None, None, past, HEAD_DIM), lambda s: (smp_bht(s)[0], layer, 0, smp_bht(s)[1])),
            pl.BlockSpec((None, None, past, HEAD_DIM), lambda s: (smp_bht(s)[0], layer, 0, smp_bht(s)[1])),
        ],
        out_specs=pl.BlockSpec((tq, gw), q_map),
        out_shape=jax.ShapeDtypeStruct((n_tok, ATTN_W), BF16),
        compiler_params=_params(("arbitrary",), 32),
        name="attention",
    )(q, kv, kv, kv, kv, cache_k, cache_v)


def _lane_major(a):
    depth, rows, width = a.shape
    return a.reshape(depth, rows, width // LANES, LANES).transpose(0, 2, 1, 3)


def _glu(a, b):
    return a * jax.nn.sigmoid(b)


def _conv_lane_tile(c, u1_ref, u2_ref, p1_ref, p2_ref, n1_ref, n2_ref, cw_ref, cb_ref, pad_ref, cvf_ref,
                    anchor, *, tile, n_ctx_tiles, tiles_per_seq):
    n_sub = pad_ref.shape[0]
    sub, halo = CONV_TILE, CONV_HALO
    is_smp = tile >= n_ctx_tiles
    pos = (tile - n_ctx_tiles) % tiles_per_seq
    has_prev = jnp.logical_and(is_smp, pos != 0)
    has_next = jnp.logical_and(is_smp, pos != tiles_per_seq - 1)

    sl = slice(c * LANES, (c + 1) * LANES)
    g = _glu(u1_ref[:, sl], u2_ref[:, sl])
    if anchor is not None:
        g = g + anchor
    for s in range(n_sub):
        if s == 0:
            lo = jnp.where(has_prev, _glu(p1_ref[:, sl], p2_ref[:, sl]), 0.0)
        else:
            lo = jnp.where(is_smp, g[s * sub - halo:s * sub], 0.0)
        if s == n_sub - 1:
            hi = jnp.where(has_next, _glu(n1_ref[:, sl], n2_ref[:, sl]), 0.0)
        else:
            hi = jnp.where(is_smp, g[(s + 1) * sub:(s + 1) * sub + halo], 0.0)
        pad_ref[s, c, 0:halo, :] = lo
        pad_ref[s, c, halo:halo + sub, :] = g[s * sub:(s + 1) * sub]
        pad_ref[s, c, halo + sub:, :] = hi

    for s in range(n_sub):
        for r0 in range(0, sub, CONV_CHUNK):
            acc = jnp.zeros((CONV_CHUNK, LANES), F32)
            for k in range(CONV_KERNEL):
                start = r0 + halo - CONV_PAD + k
                acc = acc + pad_ref[s, c, start:start + CONV_CHUNK, :] * cw_ref[c, k:k + 1, :]
            cvf_ref[c, s * sub + r0:s * sub + r0 + CONV_CHUNK, :] = acc + cb_ref[c]


def _conv_finish(cvf_ref, lg_ref, lb_ref, cv_ref):
    n_lt = cvf_ref.shape[0]
    width = n_lt * LANES
    tot = cvf_ref[0]
    for c in range(1, n_lt):
        tot = tot + cvf_ref[c]
    mu = tot.sum(axis=-1, keepdims=True) / width
    sq = jnp.zeros_like(tot)
    for c in range(n_lt):
        d = cvf_ref[c] - mu
        sq = sq + d * d
    rstd = lax.rsqrt(sq.sum(axis=-1, keepdims=True) / width + EPS)
    for c in range(n_lt):
        z = (cvf_ref[c] - mu) * rstd * lg_ref[c] + lb_ref[c]
        cv_ref[:, c * LANES:(c + 1) * LANES] = (z * jax.nn.sigmoid(z)).astype(BF16)


def _out_chunk(c, n_chunks, a_ref, cv_ref, x_ref, mod_ref, w_ref, xo_ref):
    ka = a_ref.shape[1]
    ncols = xo_ref.shape[1] // n_chunks
    cols = slice(c * ncols, (c + 1) * ncols)
    mix = jnp.dot(a_ref[...], w_ref[0:ka, cols], preferred_element_type=F32)
    mix = mix + jnp.dot(cv_ref[...], w_ref[ka:, cols], preferred_element_type=F32)
    xc = x_ref[:, cols] + mod_ref[2:3, cols] * mix
    xo_ref[:, cols] = xc
    return jnp.sum(xc * xc, axis=-1, keepdims=True), xc[0:1, 0:LANES]


def _zero_after(v):
    bits = lax.bitcast_convert_type(v, jnp.uint32)
    return lax.bitcast_convert_type((bits >> 16) >> 16, F32)


def _out_finish(ssq, xo_ref, mod_ref, g2_ref, h_ref):
    x = xo_ref[...]
    y = x * lax.rsqrt(ssq / x.shape[1] + EPS) * g2_ref[...]
    h_ref[...] = (y * (1.0 + mod_ref[4:5, :]) + mod_ref[3:4, :]).astype(BF16)


def _mix_kernel(u1_ref, u2_ref, p1_ref, p2_ref, n1_ref, n2_ref, cw_ref, cb_ref, lg_ref, lb_ref,
                a_ref, x_ref, mod_ref, g2_ref, w_ref, xo_ref, h_ref, pad_ref, cvf_ref, cur_ref, nxt_ref,
                *, n_ctx_tiles, tiles_per_seq):
    i = pl.program_id(0)
    n_tiles = pl.num_programs(0) - 1
    n_lt = cvf_ref.shape[0]

    def step(do_conv, do_out, cv_dst):
        ssq = None
        for c in range(n_lt):
            anchor = None
            if do_out:
                part, probe = _out_chunk(c, n_lt, a_ref, cur_ref, x_ref, mod_ref, w_ref, xo_ref)
                ssq = part if ssq is None else ssq + part
                anchor = _zero_after(probe)
            if do_conv:
                _conv_lane_tile(c, u1_ref, u2_ref, p1_ref, p2_ref, n1_ref, n2_ref, cw_ref, cb_ref, pad_ref,
                                cvf_ref, anchor, tile=i, n_ctx_tiles=n_ctx_tiles, tiles_per_seq=tiles_per_seq)
        if do_conv:
            _conv_finish(cvf_ref, lg_ref, lb_ref, cv_dst)
        if do_out:
            _out_finish(ssq, xo_ref, mod_ref, g2_ref, h_ref)

    @pl.when(i == 0)
    def _():
        step(True, False, cur_ref)

    @pl.when(jnp.logical_and(i > 0, i < n_tiles))
    def _():
        step(True, True, nxt_ref)
        cur_ref[...] = nxt_ref[...]

    @pl.when(i == n_tiles)
    def _():
        step(False, True, None)


def _mix_projection(u, attn, x, mod, g2, w_out, conv_w, conv_b, ln_g, ln_b, layer, *, n_ctx, seq_c, seq_s, tm=512):
    n_tok, d = x.shape
    width = u.shape[1] // 2
    n_lt = width // LANES
    ka = attn.shape[1]
    assert seq_c == CONV_TILE and tm % CONV_TILE == 0 and seq_s % tm == 0 and n_ctx % tm == 0
    n_tiles = n_tok // tm
    n_ctx_tiles = n_ctx // tm
    tiles_per_seq = seq_s // tm
    hb = tm // CONV_HALO
    last_hb = n_tok // CONV_HALO - 1
    mod_row = _mod_row_fn(n_ctx, seq_s, tm)
    conv_tile = lambda i: jnp.minimum(i, n_tiles - 1)
    out_tile = lambda i: jnp.maximum(i - 1, 0)
    prev_map = lambda c: (lambda i: (jnp.maximum(conv_tile(i) * hb - 1, 0), c))
    next_map = lambda c: (lambda i: (jnp.minimum((conv_tile(i) + 1) * hb, last_hb), c))
    vec_spec = pl.BlockSpec((None, n_lt, 1, LANES), lambda i: (layer, 0, 0, 0))
    kern = functools.partial(_mix_kernel, n_ctx_tiles=n_ctx_tiles, tiles_per_seq=tiles_per_seq)
    return pl.pallas_call(
        kern,
        grid=(n_tiles + 1,),
        in_specs=[
            pl.BlockSpec((tm, width), lambda i: (conv_tile(i), 0)),
            pl.BlockSpec((tm, width), lambda i: (conv_tile(i), 1)),
            pl.BlockSpec((CONV_HALO, width), prev_map(0)),
            pl.BlockSpec((CONV_HALO, width), prev_map(1)),
            pl.BlockSpec((CONV_HALO, width), next_map(0)),
            pl.BlockSpec((CONV_HALO, width), next_map(1)),
            pl.BlockSpec((None, n_lt, CONV_KERNEL, LANES), lambda i: (layer, 0, 0, 0)),
            vec_spec, vec_spec, vec_spec,
            pl.BlockSpec((tm, ka), lambda i: (out_tile(i), 0)),
            pl.BlockSpec((tm, d), lambda i: (out_tile(i), 0)),
            pl.BlockSpec((None, None, N_MOD, d), lambda i: (layer, mod_row(out_tile(i)), 0, 0)),
            pl.BlockSpec((None, 1, d), lambda i: (layer, 0, 0)),
            pl.BlockSpec((None, ka + width, d), lambda i: (layer, 0, 0), pipeline_mode=RESIDENT),
        ],
        out_specs=[
            pl.BlockSpec((tm, d), lambda i: (out_tile(i), 0)),
            pl.BlockSpec((tm, d), lambda i: (out_tile(i), 0)),
        ],
        out_shape=[
            jax.ShapeDtypeStruct((n_tok, d), F32),
            jax.ShapeDtypeStruct((n_tok, d), BF16),
        ],
        scratch_shapes=[
            pltpu.VMEM((tm // CONV_TILE, n_lt, CONV_TILE + 2 * CONV_HALO, LANES), F32),
            pltpu.VMEM((n_lt, tm, LANES), F32),
            pltpu.VMEM((tm, width), BF16),
            pltpu.VMEM((tm, width), BF16),
        ],
        compiler_params=_params(("arbitrary",), 56),
        name="mix_projection",
    )(u, u, u, u, u, u, conv_w, conv_b, ln_g, ln_b, attn, x, mod, g2, w_out)


def _ff1_kernel(h_ref, w_ref, a_ref, *, rows):
    w = w_ref[...].astype(BF16)
    for r0 in range(0, a_ref.shape[0], rows):
        a = jnp.dot(h_ref[r0:r0 + rows, :], w, preferred_element_type=F32)
        a_ref[r0:r0 + rows, :] = jnp.square(jnp.maximum(a, 0.0)).astype(BF16)


def _ff1(h, w1, layer, *, tm=2048, tn=1024, rows=1024):
    n_tok, d = h.shape
    d_ff = w1.shape[2]
    return pl.pallas_call(
        functools.partial(_ff1_kernel, rows=rows),
        grid=(d_ff // tn, n_tok // tm),
        in_specs=[
            pl.BlockSpec((tm, d), lambda n, m: (m, 0)),
            pl.BlockSpec((None, d, tn), lambda n, m: (layer, 0, n)),
        ],
        out_specs=pl.BlockSpec((tm, tn), lambda n, m: (m, n)),
        out_shape=jax.ShapeDtypeStruct((n_tok, d_ff), BF16),
        compiler_params=_params(("arbitrary", "arbitrary"), 56),
        name="ff1",
    )(h, w1)


def _ff2_kernel(a_ref, w_ref, x_ref, mod_ref, o_ref):
    p = jnp.dot(a_ref[...], w_ref[...], preferred_element_type=F32)
    o_ref[...] = x_ref[...] + mod_ref[5:6, :] * p


def _ff2(a, w2, x, mod, layer, tile0, n_tiles, *, n_ctx, seq_s, tm=1024, tn=256):
    d_ff, d = w2.shape[1], w2.shape[2]
    mod_row = _mod_row_fn(n_ctx, seq_s, tm)
    return pl.pallas_call(
        _ff2_kernel,
        grid=(n_tiles, d // tn),
        in_specs=[
            pl.BlockSpec((tm, d_ff), lambda m, n: (tile0 + m, 0)),
            pl.BlockSpec((None, d_ff, tn), lambda m, n: (layer, 0, n)),
            pl.BlockSpec((tm, tn), lambda m, n: (tile0 + m, n)),
            pl.BlockSpec((None, None, N_MOD, tn), lambda m, n: (layer, mod_row(tile0 + m), 0, n)),
        ],
        out_specs=pl.BlockSpec((tm, tn), lambda m, n: (m, n)),
        out_shape=jax.ShapeDtypeStruct((n_tiles * tm, d), F32),
        compiler_params=_params(("arbitrary", "arbitrary"), 52),
        name="ff2",
    )(a, w2, x, mod)


def _rotary_tables(n_tokens, n_identity):
    n_rows = n_tokens // GRID_W
    row = jnp.repeat(jnp.arange(n_rows, dtype=jnp.int32), GRID_W).astype(F32)
    col = jnp.tile(jnp.arange(GRID_W, dtype=jnp.int32), n_rows).astype(F32)
    n_pairs_axis = HEAD_DIM // 4
    freqs = ROPE_THETA ** (-jnp.arange(n_pairs_axis, dtype=F32) / n_pairs_axis)
    ang = jnp.concatenate([row[:, None] * freqs, col[:, None] * freqs], axis=-1)
    cos, sin = jnp.cos(ang), jnp.sin(ang)
    cos2 = jnp.concatenate([cos, cos], axis=-1)
    sin2 = jnp.concatenate([-sin, sin], axis=-1)
    return (jnp.concatenate([cos2, jnp.ones((n_identity, HEAD_DIM), F32)], axis=0),
            jnp.concatenate([sin2, jnp.zeros((n_identity, HEAD_DIM), F32)], axis=0))


def kernel(x_prompt, x_sample, cache_k, cache_v, c, c_ctx, norm1_g, w_mod, b_mod, w_in, q_norm_g, k_norm_g,
           conv_w, conv_b, conv_norm_g, conv_norm_b, w_out, norm2_g, w_ff1, w_ff2):
    n_cseq, seq_c, d = x_prompt.shape
    n_sseq, seq_s, _ = x_sample.shape
    depth = w_in.shape[0]
    n_ctx = n_cseq * seq_c
    n_smp = n_sseq * seq_s
    in_tm = 512
    ff_tm = 1024

    x = jnp.concatenate([x_prompt.reshape(n_ctx, d), x_sample.reshape(n_smp, d)], axis=0)
    cond8 = jnp.concatenate([c_ctx[None, :], c, jnp.zeros((8 - 1 - n_sseq, d), F32)], axis=0)
    mod = _modulation(cond8, w_mod, b_mod).reshape(depth, 8, N_MOD, d)
    cos_ext, sin_ext = _rotary_tables(seq_s, in_tm)
    ck = cache_k.reshape(n_sseq, depth, cache_k.shape[2], KV_W)
    cvv = cache_v.reshape(n_sseq, depth, cache_v.shape[2], KV_W)

    g1, g2 = norm1_g[:, None, :], norm2_g[:, None, :]
    qg, kg = q_norm_g[:, None, :], k_norm_g[:, None, :]
    conv_w_lm = _lane_major(conv_w)
    conv_b_lm, ln_g_lm, ln_b_lm = (_lane_major(a[:, None, :]) for a in (conv_b, conv_norm_g, conv_norm_b))
    w_in_b, w_out_b, w_ff2_b = (w.astype(BF16) for w in (w_in, w_out, w_ff2))
    dims = dict(n_ctx=n_ctx, seq_s=seq_s)

    new_kv = [jnp.zeros((n_cseq, depth, seq_c, KV_W), F32) for _ in range(2)]
    for l in range(depth):
        q, kv, u, *new_kv = _in_projection(x, mod, g1, w_in_b, qg, kg, cos_ext, sin_ext, new_kv, l,
                                           seq_c=seq_c, tm=in_tm, **dims)
        attn = _attention(q, kv, ck, cvv, l, n_ctx=n_ctx, seq_c=seq_c, seq_s=seq_s)
        x, h2 = _mix_projection(u, attn, x, mod, g2, w_out_b, conv_w_lm, conv_b_lm, ln_g_lm, ln_b_lm, l,
                                seq_c=seq_c, **dims)
        a = _ff1(h2, w_ff1, l)
        if l < depth - 1:
            x = _ff2(a, w_ff2_b, x, mod, l, 0, (n_ctx + n_smp) // ff_tm, tm=ff_tm, **dims)
        else:
            y_prompt = _ff2(a, w_ff2_b, x, mod, l, 0, n_ctx // ff_tm, tm=ff_tm, **dims)
            y_sample = _ff2(a, w_ff2_b, x, mod, l, n_ctx // ff_tm, n_smp // ff_tm, tm=ff_tm, **dims)

    new_k, new_v = new_kv
    kv_shape = (n_cseq, depth, seq_c, N_KV_HEADS, HEAD_DIM)
    return (y_prompt.reshape(n_cseq, seq_c, d), y_sample.reshape(n_sseq, seq_s, d),
            new_k.reshape(kv_shape), new_v.reshape(kv_shape))
```

```python
import functools

import jax
import jax.numpy as jnp
from jax import lax
from jax.experimental import pallas as pl
from jax.experimental.pallas import tpu as pltpu

F32 = jnp.float32
BF16 = jnp.bfloat16

HEAD_DIM = 128
N_Q_HEADS = 8
N_KV_HEADS = 2
Q_PER_KV = N_Q_HEADS // N_KV_HEADS
ATTN_W = N_Q_HEADS * HEAD_DIM
KV_W = N_KV_HEADS * HEAD_DIM
GRID_W = 64
CONV_KERNEL = 31
CONV_PAD = CONV_KERNEL // 2
N_MOD = 6
ROPE_THETA = 10000.0
EPS = 1e-6
LOG2E = 1.4426950408889634

LANES = 128
CONV_HALO = 16
CONV_TILE = 256
CONV_CHUNK = 32
MIB = 1024 * 1024
RESIDENT = pl.Buffered(1)


def _params(semantics, vmem_mib):
    return pltpu.CompilerParams(dimension_semantics=semantics, vmem_limit_bytes=vmem_mib * MIB)


def _mod_row_fn(n_ctx, seq_s, tm):
    n_ctx_tiles = n_ctx // tm
    tiles_per_seq = seq_s // tm
    assert n_ctx % tm == 0 and seq_s % tm == 0
    return lambda i: jnp.where(i < n_ctx_tiles, 0, 1 + (i - n_ctx_tiles) // tiles_per_seq)


def _mod_kernel(c_ref, w_ref, b_ref, o_ref):
    a = jax.nn.silu(c_ref[...]).astype(BF16)
    o_ref[...] = jnp.dot(a, w_ref[...].astype(BF16), preferred_element_type=F32) + b_ref[...]


def _modulation(cond8, w_mod, b_mod, tn=1024):
    depth, d, n = w_mod.shape
    return pl.pallas_call(
        _mod_kernel,
        grid=(depth, n // tn),
        in_specs=[
            pl.BlockSpec((8, d), lambda l, j: (0, 0)),
            pl.BlockSpec((None, d, tn), lambda l, j: (l, 0, j)),
            pl.BlockSpec((None, 1, tn), lambda l, j: (l, 0, j)),
        ],
        out_specs=pl.BlockSpec((None, 8, tn), lambda l, j: (l, 0, j)),
        out_shape=jax.ShapeDtypeStruct((depth, 8, n), F32),
        compiler_params=_params(("arbitrary", "arbitrary"), 40),
        name="modulation",
    )(cond8, w_mod, b_mod.reshape(depth, 1, n))


def _rms_heads(acc, g, n_heads, cos2, sin2):
    outs = []
    for h in range(n_heads):
        y = acc[:, h * HEAD_DIM:(h + 1) * HEAD_DIM]
        y = y * lax.rsqrt(jnp.mean(y * y, axis=-1, keepdims=True) + EPS) * g
        outs.append(y * cos2 + pltpu.roll(y, HEAD_DIM // 2, 1) * sin2)
    return outs


def _in_kernel(x_ref, mod_ref, g1_ref, w_ref, qg_ref, kg_ref, cos_ref, sin_ref, wo_ref, nk_in, nv_in,
               q_ref, kv_ref, u_ref, nk_ref, nv_ref, wob_ref, *, n_ctx_tiles, tn):
    del nk_in, nv_in
    wob_ref[...] = wo_ref[...].astype(BF16)
    x = x_ref[...]
    y = x * lax.rsqrt(jnp.mean(x * x, axis=-1, keepdims=True) + EPS) * g1_ref[...]
    h = (y * (1.0 + mod_ref[1:2, :]) + mod_ref[0:1, :]).astype(BF16)
    cos2, sin2 = cos_ref[...], sin_ref[...]

    def proj(c0, width):
        return jnp.dot(h, w_ref[:, c0:c0 + width], preferred_element_type=F32)

    for c0 in range(0, ATTN_W, tn):
        ys = _rms_heads(proj(c0, tn), qg_ref[...], tn // HEAD_DIM, cos2, sin2)
        for hh, yv in enumerate(ys):
            q_ref[:, c0 + hh * HEAD_DIM:c0 + (hh + 1) * HEAD_DIM] = yv.astype(BF16)

    acc = proj(ATTN_W, 2 * KV_W)
    for hh, yv in enumerate(_rms_heads(acc[:, :KV_W], kg_ref[...], N_KV_HEADS, cos2, sin2)):
        kv_ref[:, hh * HEAD_DIM:(hh + 1) * HEAD_DIM] = yv
    kv_ref[:, KV_W:] = acc[:, KV_W:]

    u0 = ATTN_W + 2 * KV_W
    for c0 in range(0, u_ref.shape[1], tn):
        u_ref[:, c0:c0 + tn] = proj(u0 + c0, tn)

    @pl.when(pl.program_id(0) < n_ctx_tiles)
    def _():
        seq = nk_ref.shape[1]
        for s in range(nk_ref.shape[0]):
            nk_ref[s] = kv_ref[s * seq:(s + 1) * seq, :KV_W]
            nv_ref[s] = kv_ref[s * seq:(s + 1) * seq, KV_W:]


def _in_projection(x, mod, g1, w_in, qg, kg, cos_ext, sin_ext, w_out, new_kv, layer, *, n_ctx, seq_c, seq_s,
                   tm=512, tn=512):
    n_tok, d = x.shape
    depth, _, n_cols = w_in.shape
    wo_rows = w_out.shape[1] // (n_tok // tm)
    assert w_out.shape[1] % (n_tok // tm) == 0 and w_out.shape[2] == d
    n_cseq = n_ctx // seq_c
    n_ctx_tiles = n_ctx // tm
    tiles_per_seq = seq_s // tm
    mod_row = _mod_row_fn(n_ctx, seq_s, tm)
    assert tm % seq_c == 0
    nkv_shape = (n_cseq, depth, seq_c, KV_W)
    nkv_spec = pl.BlockSpec((tm // seq_c, None, seq_c, KV_W),
                            lambda i: (jnp.minimum(i, n_ctx_tiles - 1), layer, 0, 0))
    rot_spec = pl.BlockSpec((tm, HEAD_DIM),
                            lambda i: (jnp.where(i < n_ctx_tiles, tiles_per_seq, i % tiles_per_seq), 0))
    head_spec = pl.BlockSpec((None, 1, HEAD_DIM), lambda i: (layer, 0, 0))
    in_specs = [
        pl.BlockSpec((tm, d), lambda i: (i, 0)),
        pl.BlockSpec((None, None, N_MOD, d), lambda i: (layer, mod_row(i), 0, 0)),
        pl.BlockSpec((None, 1, d), lambda i: (layer, 0, 0)),
        pl.BlockSpec((None, d, n_cols), lambda i: (layer, 0, 0), pipeline_mode=RESIDENT),
        head_spec, head_spec, rot_spec, rot_spec,
        pl.BlockSpec((None, wo_rows, d), lambda i: (layer, i, 0)),
        pl.BlockSpec(memory_space=pl.ANY),
        pl.BlockSpec(memory_space=pl.ANY),
    ]
    args = [x, mod, g1, w_in, qg, kg, cos_ext, sin_ext, w_out, *new_kv]
    aliases = {len(args) - 2: 3, len(args) - 1: 4}
    return pl.pallas_call(
        functools.partial(_in_kernel, n_ctx_tiles=n_ctx_tiles, tn=tn),
        grid=(n_tok // tm,),
        in_specs=in_specs,
        out_specs=[
            pl.BlockSpec((tm, ATTN_W), lambda i: (i, 0)),
            pl.BlockSpec((tm, 2 * KV_W), lambda i: (i, 0)),
            pl.BlockSpec((tm, n_cols - ATTN_W - 2 * KV_W), lambda i: (i, 0)),
            nkv_spec,
            nkv_spec,
            pl.BlockSpec((wo_rows, d), lambda i: (i, 0)),
        ],
        out_shape=[
            jax.ShapeDtypeStruct((n_tok, ATTN_W), BF16),
            jax.ShapeDtypeStruct((n_tok, 2 * KV_W), F32),
            jax.ShapeDtypeStruct((n_tok, n_cols - ATTN_W - 2 * KV_W), F32),
            jax.ShapeDtypeStruct(nkv_shape, F32),
            jax.ShapeDtypeStruct(nkv_shape, F32),
            jax.ShapeDtypeStruct(w_out.shape[1:], BF16),
        ],
        input_output_aliases=aliases,
        compiler_params=_params(("arbitrary",), 56),
        name="in_projection",
    )(*args)


def _softmax_pv(q, keys, vals, scale):
    dn = (((1,), (1,)), ((), ()))
    ss = [lax.dot_general(q, k, dn, preferred_element_type=F32) for k in keys]
    m = ss[0].max(axis=-1, keepdims=True)
    for s in ss[1:]:
        m = jnp.maximum(m, s.max(axis=-1, keepdims=True))
    es = [jnp.exp2((s - m) * (scale * LOG2E)) for s in ss]
    den = es[0].sum(axis=-1, keepdims=True)
    for e in es[1:]:
        den = den + e.sum(axis=-1, keepdims=True)
    o = jnp.dot(es[0].astype(BF16), vals[0], preferred_element_type=F32)
    for e, v in zip(es[1:], vals[1:]):
        o = o + jnp.dot(e.astype(BF16), v, preferred_element_type=F32)
    return o / den


def _attn_kernel(q_ref, kc_ref, vc_ref, ks_ref, vs_ref, ck_ref, cv_ref, o_ref, *, n_ctx_steps):
    scale = HEAD_DIM ** -0.5
    is_ctx = pl.program_id(0) < n_ctx_steps

    def attend(key_refs, val_refs):
        keys = [r[...].astype(BF16) for r in key_refs]
        vals = [r[...].astype(BF16) for r in val_refs]
        for g in range(Q_PER_KV):
            sl = slice(g * HEAD_DIM, (g + 1) * HEAD_DIM)
            o_ref[:, sl] = _softmax_pv(q_ref[:, sl], keys, vals, scale).astype(BF16)

    pl.when(is_ctx)(lambda: attend([kc_ref], [vc_ref]))
    pl.when(jnp.logical_not(is_ctx))(lambda: attend([ks_ref, ck_ref], [vs_ref, cv_ref]))


def _attention(q, kv, cache_k, cache_v, layer, *, n_ctx, seq_c, seq_s):
    gw = Q_PER_KV * HEAD_DIM
    tq = seq_c
    n_tok = q.shape[0]
    n_cseq = n_ctx // seq_c
    n_sseq = (n_tok - n_ctx) // seq_s
    past = cache_k.shape[2]
    n_qt = seq_s // tq
    n_ctx_steps = n_cseq * N_KV_HEADS
    n_smp_steps = n_sseq * N_KV_HEADS * n_qt
    q0 = n_ctx // tq
    s0 = n_ctx // seq_s
    assert n_ctx % seq_s == 0

    def ctx_bh(s):
        s = jnp.minimum(s, n_ctx_steps - 1)
        return s // N_KV_HEADS, s % N_KV_HEADS

    def smp_bht(s):
        r = jnp.maximum(s - n_ctx_steps, 0)
        return r // (N_KV_HEADS * n_qt), (r // n_qt) % N_KV_HEADS, r % n_qt

    def q_map(s):
        cb, ch = ctx_bh(s)
        sb, sh, st = smp_bht(s)
        is_ctx = s < n_ctx_steps
        return jnp.where(is_ctx, cb, q0 + sb * n_qt + st), jnp.where(is_ctx, ch, sh)

    return pl.pallas_call(
        functools.partial(_attn_kernel, n_ctx_steps=n_ctx_steps),
        grid=(n_ctx_steps + n_smp_steps,),
        in_specs=[
            pl.BlockSpec((tq, gw), q_map),
            pl.BlockSpec((seq_c, HEAD_DIM), lambda s: ctx_bh(s)),
            pl.BlockSpec((seq_c, HEAD_DIM), lambda s: (ctx_bh(s)[0], N_KV_HEADS + ctx_bh(s)[1])),
            pl.BlockSpec((seq_s, HEAD_DIM), lambda s: (s0 + smp_bht(s)[0], smp_bht(s)[1])),
            pl.BlockSpec((seq_s, HEAD_DIM), lambda s: (s0 + smp_bht(s)[0], N_KV_HEADS + smp_bht(s)[1])),
            pl.BlockSpec((None, None, past, HEAD_DIM), lambda s: (smp_bht(s)[0], layer, 0, smp_bht(s)[1])),
            pl.BlockSpec((None, None, past, HEAD_DIM), lambda s: (smp_bht(s)[0], layer, 0, smp_bht(s)[1])),
        ],
        out_specs=pl.BlockSpec((tq, gw), q_map),
        out_shape=jax.ShapeDtypeStruct((n_tok, ATTN_W), BF16),
        compiler_params=_params(("arbitrary",), 32),
        name="attention",
    )(q, kv, kv, kv, kv, cache_k, cache_v)


def _lane_major(a):
    depth, rows, width = a.shape
    return a.reshape(depth, rows, width // LANES, LANES).transpose(0, 2, 1, 3)


def _glu(a, b):
    return a * jax.nn.sigmoid(b)


def _conv_lane_tile(c, u1_ref, u2_ref, p1_ref, p2_ref, n1_ref, n2_ref, cw_ref, cb_ref, pad_ref, cvf_ref,
                    anchor, *, tile, n_ctx_tiles, tiles_per_seq):
    n_sub = pad_ref.shape[0]
    sub, halo = CONV_TILE, CONV_HALO
    is_smp = tile >= n_ctx_tiles
    pos = (tile - n_ctx_tiles) % tiles_per_seq
    has_prev = jnp.logical_and(is_smp, pos != 0)
    has_next = jnp.logical_and(is_smp, pos != tiles_per_seq - 1)

    sl = slice(c * LANES, (c + 1) * LANES)
    g = _glu(u1_ref[:, sl], u2_ref[:, sl])
    if anchor is not None:
        g = g + anchor
    for s in range(n_sub):
        if s == 0:
            lo = jnp.where(has_prev, _glu(p1_ref[:, sl], p2_ref[:, sl]), 0.0)
        else:
            lo = jnp.where(is_smp, g[s * sub - halo:s * sub], 0.0)
        if s == n_sub - 1:
            hi = jnp.where(has_next, _glu(n1_ref[:, sl], n2_ref[:, sl]), 0.0)
        else:
            hi = jnp.where(is_smp, g[(s + 1) * sub:(s + 1) * sub + halo], 0.0)
        pad_ref[s, c, 0:halo, :] = lo
        pad_ref[s, c, halo:halo + sub, :] = g[s * sub:(s + 1) * sub]
        pad_ref[s, c, halo + sub:, :] = hi

    for s in range(n_sub):
        for r0 in range(0, sub, CONV_CHUNK):
            acc = jnp.zeros((CONV_CHUNK, LANES), F32)
            for k in range(CONV_KERNEL):
                start = r0 + halo - CONV_PAD + k
                acc = acc + pad_ref[s, c, start:start + CONV_CHUNK, :] * cw_ref[c, k:k + 1, :]
            cvf_ref[c, s * sub + r0:s * sub + r0 + CONV_CHUNK, :] = acc + cb_ref[c]


def _conv_finish(cvf_ref, lg_ref, lb_ref, cv_ref):
    n_lt = cvf_ref.shape[0]
    width = n_lt * LANES
    tot = cvf_ref[0]
    for c in range(1, n_lt):
        tot = tot + cvf_ref[c]
    mu = tot.sum(axis=-1, keepdims=True) / width
    sq = jnp.zeros_like(tot)
    for c in range(n_lt):
        d = cvf_ref[c] - mu
        sq = sq + d * d
    rstd = lax.rsqrt(sq.sum(axis=-1, keepdims=True) / width + EPS)
    for c in range(n_lt):
        z = (cvf_ref[c] - mu) * rstd * lg_ref[c] + lb_ref[c]
        cv_ref[:, c * LANES:(c + 1) * LANES] = (z * jax.nn.sigmoid(z)).astype(BF16)


def _out_chunk(c, n_chunks, a_ref, cv_ref, x_ref, mod_ref, w_ref, xo_ref):
    ka = a_ref.shape[1]
    ncols = xo_ref.shape[1] // n_chunks
    cols = slice(c * ncols, (c + 1) * ncols)
    mix = jnp.dot(a_ref[...], w_ref[0:ka, cols], preferred_element_type=F32)
    mix = mix + jnp.dot(cv_ref[...], w_ref[ka:, cols], preferred_element_type=F32)
    xc = x_ref[:, cols] + mod_ref[2:3, cols] * mix
    xo_ref[:, cols] = xc
    return jnp.sum(xc * xc, axis=-1, keepdims=True), xc[0:1, 0:LANES]


def _zero_after(v):
    bits = lax.bitcast_convert_type(v, jnp.uint32)
    return lax.bitcast_convert_type((bits >> 16) >> 16, F32)


def _out_finish(ssq, xo_ref, mod_ref, g2_ref, h_ref):
    x = xo_ref[...]
    y = x * lax.rsqrt(ssq / x.shape[1] + EPS) * g2_ref[...]
    h_ref[...] = (y * (1.0 + mod_ref[4:5, :]) + mod_ref[3:4, :]).astype(BF16)


def _mix_kernel(u1_ref, u2_ref, p1_ref, p2_ref, n1_ref, n2_ref, cw_ref, cb_ref, lg_ref, lb_ref,
                a_ref, x_ref, mod_ref, g2_ref, w_ref, xo_ref, h_ref, pad_ref, cvf_ref, cur_ref, nxt_ref,
                *, n_ctx_tiles, tiles_per_seq):
    i = pl.program_id(0)
    n_tiles = pl.num_programs(0) - 1
    n_lt = cvf_ref.shape[0]

    def step(do_conv, do_out, cv_dst):
        ssq = None
        for c in range(n_lt):
            anchor = None
            if do_out:
                part, probe = _out_chunk(c, n_lt, a_ref, cur_ref, x_ref, mod_ref, w_ref, xo_ref)
                ssq = part if ssq is None else ssq + part
                anchor = _zero_after(probe)
            if do_conv:
                _conv_lane_tile(c, u1_ref, u2_ref, p1_ref, p2_ref, n1_ref, n2_ref, cw_ref, cb_ref, pad_ref,
                                cvf_ref, anchor, tile=i, n_ctx_tiles=n_ctx_tiles, tiles_per_seq=tiles_per_seq)
        if do_conv:
            _conv_finish(cvf_ref, lg_ref, lb_ref, cv_dst)
        if do_out:
            _out_finish(ssq, xo_ref, mod_ref, g2_ref, h_ref)

    @pl.when(i == 0)
    def _():
        step(True, False, cur_ref)

    @pl.when(jnp.logical_and(i > 0, i < n_tiles))
    def _():
        step(True, True, nxt_ref)
        cur_ref[...] = nxt_ref[...]

    @pl.when(i == n_tiles)
    def _():
        step(False, True, None)


def _mix_projection(u, attn, x, mod, g2, w_out, conv_w, conv_b, ln_g, ln_b, layer, *, n_ctx, seq_c, seq_s, tm=512):
    n_tok, d = x.shape
    width = u.shape[1] // 2
    n_lt = width // LANES
    ka = attn.shape[1]
    assert seq_c == CONV_TILE and tm % CONV_TILE == 0 and seq_s % tm == 0 and n_ctx % tm == 0
    n_tiles = n_tok // tm
    n_ctx_tiles = n_ctx // tm
    tiles_per_seq = seq_s // tm
    hb = tm // CONV_HALO
    last_hb = n_tok // CONV_HALO - 1
    mod_row = _mod_row_fn(n_ctx, seq_s, tm)
    conv_tile = lambda i: jnp.minimum(i, n_tiles - 1)
    out_tile = lambda i: jnp.maximum(i - 1, 0)
    prev_map = lambda c: (lambda i: (jnp.maximum(conv_tile(i) * hb - 1, 0), c))
    next_map = lambda c: (lambda i: (jnp.minimum((conv_tile(i) + 1) * hb, last_hb), c))
    vec_spec = pl.BlockSpec((None, n_lt, 1, LANES), lambda i: (layer, 0, 0, 0))
    kern = functools.partial(_mix_kernel, n_ctx_tiles=n_ctx_tiles, tiles_per_seq=tiles_per_seq)
    return pl.pallas_call(
        kern,
        grid=(n_tiles + 1,),
        in_specs=[
            pl.BlockSpec((tm, width), lambda i: (conv_tile(i), 0)),
            pl.BlockSpec((tm, width), lambda i: (conv_tile(i), 1)),
            pl.BlockSpec((CONV_HALO, width), prev_map(0)),
            pl.BlockSpec((CONV_HALO, width), prev_map(1)),
            pl.BlockSpec((CONV_HALO, width), next_map(0)),
            pl.BlockSpec((CONV_HALO, width), next_map(1)),
            pl.BlockSpec((None, n_lt, CONV_KERNEL, LANES), lambda i: (layer, 0, 0, 0)),
            vec_spec, vec_spec, vec_spec,
            pl.BlockSpec((tm, ka), lambda i: (out_tile(i), 0)),
            pl.BlockSpec((tm, d), lambda i: (out_tile(i), 0)),
            pl.BlockSpec((None, None, N_MOD, d), lambda i: (layer, mod_row(out_tile(i)), 0, 0)),
            pl.BlockSpec((None, 1, d), lambda i: (layer, 0, 0)),
            pl.BlockSpec((ka + width, d), lambda i: (0, 0), pipeline_mode=RESIDENT),
        ],
        out_specs=[
            pl.BlockSpec((tm, d), lambda i: (out_tile(i), 0)),
            pl.BlockSpec((tm, d), lambda i: (out_tile(i), 0)),
        ],
        out_shape=[
            jax.ShapeDtypeStruct((n_tok, d), F32),
            jax.ShapeDtypeStruct((n_tok, d), BF16),
        ],
        scratch_shapes=[
            pltpu.VMEM((tm // CONV_TILE, n_lt, CONV_TILE + 2 * CONV_HALO, LANES), F32),
            pltpu.VMEM((n_lt, tm, LANES), F32),
            pltpu.VMEM((tm, width), BF16),
            pltpu.VMEM((tm, width), BF16),
        ],
        compiler_params=_params(("arbitrary",), 56),
        name="mix_projection",
    )(u, u, u, u, u, u, conv_w, conv_b, ln_g, ln_b, attn, x, mod, g2, w_out)


def _ff1_kernel(h_ref, w_ref, w2_ref, a_ref, w2b_ref, *, rows):
    w2b_ref[...] = w2_ref[...].astype(BF16)
    w = w_ref[...].astype(BF16)
    for r0 in range(0, a_ref.shape[0], rows):
        a = jnp.dot(h_ref[r0:r0 + rows, :], w, preferred_element_type=F32)
        a_ref[r0:r0 + rows, :] = jnp.square(jnp.maximum(a, 0.0)).astype(BF16)


def _ff1(h, w1, w2, layer, *, tm=2048, tn=1024, rows=1024):
    n_tok, d = h.shape
    d_ff = w1.shape[2]
    n_m = n_tok // tm
    n_steps = (d_ff // tn) * n_m
    w2_rows = d_ff // n_steps
    assert d_ff % n_steps == 0
    return pl.pallas_call(
        functools.partial(_ff1_kernel, rows=rows),
        grid=(d_ff // tn, n_m),
        in_specs=[
            pl.BlockSpec((tm, d), lambda n, m: (m, 0)),
            pl.BlockSpec((None, d, tn), lambda n, m: (layer, 0, n)),
            pl.BlockSpec((None, w2_rows, d), lambda n, m: (layer, n * n_m + m, 0)),
        ],
        out_specs=[
            pl.BlockSpec((tm, tn), lambda n, m: (m, n)),
            pl.BlockSpec((w2_rows, d), lambda n, m: (n * n_m + m, 0)),
        ],
        out_shape=[
            jax.ShapeDtypeStruct((n_tok, d_ff), BF16),
            jax.ShapeDtypeStruct((d_ff, d), BF16),
        ],
        compiler_params=_params(("arbitrary", "arbitrary"), 56),
        name="ff1",
    )(h, w1, w2)


def _ff2_kernel(a_ref, w_ref, x_ref, mod_ref, o_ref):
    p = jnp.dot(a_ref[...], w_ref[...], preferred_element_type=F32)
    o_ref[...] = x_ref[...] + mod_ref[5:6, :] * p


def _ff2(a, w2, x, mod, layer, tile0, n_tiles, *, n_ctx, seq_s, tm=1024, tn=256):
    d_ff, d = w2.shape
    mod_row = _mod_row_fn(n_ctx, seq_s, tm)
    return pl.pallas_call(
        _ff2_kernel,
        grid=(n_tiles, d // tn),
        in_specs=[
            pl.BlockSpec((tm, d_ff), lambda m, n: (tile0 + m, 0)),
            pl.BlockSpec((d_ff, tn), lambda m, n: (0, n)),
            pl.BlockSpec((tm, tn), lambda m, n: (tile0 + m, n)),
            pl.BlockSpec((None, None, N_MOD, tn), lambda m, n: (layer, mod_row(tile0 + m), 0, n)),
        ],
        out_specs=pl.BlockSpec((tm, tn), lambda m, n: (m, n)),
        out_shape=jax.ShapeDtypeStruct((n_tiles * tm, d), F32),
        compiler_params=_params(("arbitrary", "arbitrary"), 52),
        name="ff2",
    )(a, w2, x, mod)


def _rotary_tables(n_tokens, n_identity):
    n_rows = n_tokens // GRID_W
    row = jnp.repeat(jnp.arange(n_rows, dtype=jnp.int32), GRID_W).astype(F32)
    col = jnp.tile(jnp.arange(GRID_W, dtype=jnp.int32), n_rows).astype(F32)
    n_pairs_axis = HEAD_DIM // 4
    freqs = ROPE_THETA ** (-jnp.arange(n_pairs_axis, dtype=F32) / n_pairs_axis)
    ang = jnp.concatenate([row[:, None] * freqs, col[:, None] * freqs], axis=-1)
    cos, sin = jnp.cos(ang), jnp.sin(ang)
    cos2 = jnp.concatenate([cos, cos], axis=-1)
    sin2 = jnp.concatenate([-sin, sin], axis=-1)
    return (jnp.concatenate([cos2, jnp.ones((n_identity, HEAD_DIM), F32)], axis=0),
            jnp.concatenate([sin2, jnp.zeros((n_identity, HEAD_DIM), F32)], axis=0))


def kernel(x_prompt, x_sample, cache_k, cache_v, c, c_ctx, norm1_g, w_mod, b_mod, w_in, q_norm_g, k_norm_g,
           conv_w, conv_b, conv_norm_g, conv_norm_b, w_out, norm2_g, w_ff1, w_ff2):
    n_cseq, seq_c, d = x_prompt.shape
    n_sseq, seq_s, _ = x_sample.shape
    depth = w_in.shape[0]
    n_ctx = n_cseq * seq_c
    n_smp = n_sseq * seq_s
    in_tm = 512
    ff_tm = 1024

    x = jnp.concatenate([x_prompt.reshape(n_ctx, d), x_sample.reshape(n_smp, d)], axis=0)
    cond8 = jnp.concatenate([c_ctx[None, :], c, jnp.zeros((8 - 1 - n_sseq, d), F32)], axis=0)
    mod = _modulation(cond8, w_mod, b_mod).reshape(depth, 8, N_MOD, d)
    cos_ext, sin_ext = _rotary_tables(seq_s, in_tm)
    ck = cache_k.reshape(n_sseq, depth, cache_k.shape[2], KV_W)
    cvv = cache_v.reshape(n_sseq, depth, cache_v.shape[2], KV_W)

    g1, g2 = norm1_g[:, None, :], norm2_g[:, None, :]
    qg, kg = q_norm_g[:, None, :], k_norm_g[:, None, :]
    conv_w_lm = _lane_major(conv_w)
    conv_b_lm, ln_g_lm, ln_b_lm = (_lane_major(a[:, None, :]) for a in (conv_b, conv_norm_g, conv_norm_b))
    w_in_b = w_in.astype(BF16)
    dims = dict(n_ctx=n_ctx, seq_s=seq_s)

    new_kv = [jnp.zeros((n_cseq, depth, seq_c, KV_W), F32) for _ in range(2)]
    for l in range(depth):
        q, kv, u, *new_kv, w_out_b = _in_projection(x, mod, g1, w_in_b, qg, kg, cos_ext, sin_ext, w_out, new_kv, l,
                                                    seq_c=seq_c, tm=in_tm, **dims)
        attn = _attention(q, kv, ck, cvv, l, n_ctx=n_ctx, seq_c=seq_c, seq_s=seq_s)
        x, h2 = _mix_projection(u, attn, x, mod, g2, w_out_b, conv_w_lm, conv_b_lm, ln_g_lm, ln_b_lm, l,
                                seq_c=seq_c, **dims)
        a, w_ff2_b = _ff1(h2, w_ff1, w_ff2, l)
        if l < depth - 1:
            x = _ff2(a, w_ff2_b, x, mod, l, 0, (n_ctx + n_smp) // ff_tm, tm=ff_tm, **dims)
        else:
            y_prompt = _ff2(a, w_ff2_b, x, mod, l, 0, n_ctx // ff_tm, tm=ff_tm, **dims)
            y_sample = _ff2(a, w_ff2_b, x, mod, l, n_ctx // ff_tm, n_smp // ff_tm, tm=ff_tm, **dims)

    new_k, new_v = new_kv
    kv_shape = (n_cseq, depth, seq_c, N_KV_HEADS, HEAD_DIM)
    return (y_prompt.reshape(n_cseq, seq_c, d), y_sample.reshape(n_sseq, seq_s, d),
            new_k.reshape(kv_shape), new_v.reshape(kv_shape))
```

```python
import functools

import jax
import jax.numpy as jnp
from jax import lax
from jax.experimental import pallas as pl
from jax.experimental.pallas import tpu as pltpu

F32 = jnp.float32
BF16 = jnp.bfloat16

HEAD_DIM = 128
N_Q_HEADS = 8
N_KV_HEADS = 2
Q_PER_KV = N_Q_HEADS // N_KV_HEADS
ATTN_W = N_Q_HEADS * HEAD_DIM
KV_W = N_KV_HEADS * HEAD_DIM
GRID_W = 64
CONV_KERNEL = 31
CONV_PAD = CONV_KERNEL // 2
N_MOD = 6
ROPE_THETA = 10000.0
EPS = 1e-6
LOG2E = 1.4426950408889634

LANES = 128
CONV_HALO = 16
CONV_TILE = 256
CONV_CHUNK = 32
MIB = 1024 * 1024
RESIDENT = pl.Buffered(1)


def _params(semantics, vmem_mib):
    return pltpu.CompilerParams(dimension_semantics=semantics, vmem_limit_bytes=vmem_mib * MIB)


def _mod_row_fn(n_ctx, seq_s, tm):
    n_ctx_tiles = n_ctx // tm
    tiles_per_seq = seq_s // tm
    assert n_ctx % tm == 0 and seq_s % tm == 0
    return lambda i: jnp.where(i < n_ctx_tiles, 0, 1 + (i - n_ctx_tiles) // tiles_per_seq)


def _mod_kernel(c_ref, w_ref, b_ref, o_ref):
    a = jax.nn.silu(c_ref[...]).astype(BF16)
    o_ref[...] = jnp.dot(a, w_ref[...].astype(BF16), preferred_element_type=F32) + b_ref[...]


def _modulation(cond8, w_mod, b_mod, tn=1024):
    depth, d, n = w_mod.shape
    return pl.pallas_call(
        _mod_kernel,
        grid=(depth, n // tn),
        in_specs=[
            pl.BlockSpec((8, d), lambda l, j: (0, 0)),
            pl.BlockSpec((None, d, tn), lambda l, j: (l, 0, j)),
            pl.BlockSpec((None, 1, tn), lambda l, j: (l, 0, j)),
        ],
        out_specs=pl.BlockSpec((None, 8, tn), lambda l, j: (l, 0, j)),
        out_shape=jax.ShapeDtypeStruct((depth, 8, n), F32),
        compiler_params=_params(("arbitrary", "arbitrary"), 40),
        name="modulation",
    )(cond8, w_mod, b_mod.reshape(depth, 1, n))


def _rms_heads(acc, g, n_heads, cos2, sin2):
    outs = []
    for h in range(n_heads):
        y = acc[:, h * HEAD_DIM:(h + 1) * HEAD_DIM]
        y = y * lax.rsqrt(jnp.mean(y * y, axis=-1, keepdims=True) + EPS) * g
        outs.append(y * cos2 + pltpu.roll(y, HEAD_DIM // 2, 1) * sin2)
    return outs


def _in_kernel(x_ref, mod_ref, g1_ref, w_ref, qg_ref, kg_ref, cos_ref, sin_ref, wo_ref, nk_in, nv_in,
               q_ref, kv_ref, u_ref, nk_ref, nv_ref, wob_ref, *, n_ctx_tiles, tn):
    del nk_in, nv_in
    wob_ref[...] = wo_ref[...].astype(BF16)
    x = x_ref[...]
    y = x * lax.rsqrt(jnp.mean(x * x, axis=-1, keepdims=True) + EPS) * g1_ref[...]
    h = (y * (1.0 + mod_ref[1:2, :]) + mod_ref[0:1, :]).astype(BF16)
    cos2, sin2 = cos_ref[...], sin_ref[...]

    def proj(c0, width):
        return jnp.dot(h, w_ref[:, c0:c0 + width], preferred_element_type=F32)

    for c0 in range(0, ATTN_W, tn):
        ys = _rms_heads(proj(c0, tn), qg_ref[...], tn // HEAD_DIM, cos2, sin2)
        for hh, yv in enumerate(ys):
            q_ref[:, c0 + hh * HEAD_DIM:c0 + (hh + 1) * HEAD_DIM] = yv.astype(BF16)

    acc = proj(ATTN_W, 2 * KV_W)
    for hh, yv in enumerate(_rms_heads(acc[:, :KV_W], kg_ref[...], N_KV_HEADS, cos2, sin2)):
        kv_ref[:, hh * HEAD_DIM:(hh + 1) * HEAD_DIM] = yv
    kv_ref[:, KV_W:] = acc[:, KV_W:]

    u0 = ATTN_W + 2 * KV_W
    for c0 in range(0, u_ref.shape[1], tn):
        u_ref[:, c0:c0 + tn] = proj(u0 + c0, tn)

    @pl.when(pl.program_id(0) < n_ctx_tiles)
    def _():
        seq = nk_ref.shape[1]
        for s in range(nk_ref.shape[0]):
            nk_ref[s] = kv_ref[s * seq:(s + 1) * seq, :KV_W]
            nv_ref[s] = kv_ref[s * seq:(s + 1) * seq, KV_W:]


def _in_projection(x, mod, g1, w_in, qg, kg, cos_ext, sin_ext, w_out, new_kv, layer, *, n_ctx, seq_c, seq_s,
                   tm=512, tn=512):
    n_tok, d = x.shape
    n_cols = w_in.shape[1]
    depth = w_out.shape[0]
    wo_rows = w_out.shape[1] // (n_tok // tm)
    assert w_out.shape[1] % (n_tok // tm) == 0 and w_out.shape[2] == d
    n_cseq = n_ctx // seq_c
    n_ctx_tiles = n_ctx // tm
    tiles_per_seq = seq_s // tm
    mod_row = _mod_row_fn(n_ctx, seq_s, tm)
    assert tm % seq_c == 0
    nkv_shape = (n_cseq, depth, seq_c, KV_W)
    nkv_spec = pl.BlockSpec((tm // seq_c, None, seq_c, KV_W),
                            lambda i: (jnp.minimum(i, n_ctx_tiles - 1), layer, 0, 0))
    rot_spec = pl.BlockSpec((tm, HEAD_DIM),
                            lambda i: (jnp.where(i < n_ctx_tiles, tiles_per_seq, i % tiles_per_seq), 0))
    head_spec = pl.BlockSpec((None, 1, HEAD_DIM), lambda i: (layer, 0, 0))
    in_specs = [
        pl.BlockSpec((tm, d), lambda i: (i, 0)),
        pl.BlockSpec((None, None, N_MOD, d), lambda i: (layer, mod_row(i), 0, 0)),
        pl.BlockSpec((None, 1, d), lambda i: (layer, 0, 0)),
        pl.BlockSpec((d, n_cols), lambda i: (0, 0), pipeline_mode=RESIDENT),
        head_spec, head_spec, rot_spec, rot_spec,
        pl.BlockSpec((None, wo_rows, d), lambda i: (layer, i, 0)),
        pl.BlockSpec(memory_space=pl.ANY),
        pl.BlockSpec(memory_space=pl.ANY),
    ]
    args = [x, mod, g1, w_in, qg, kg, cos_ext, sin_ext, w_out, *new_kv]
    aliases = {len(args) - 2: 3, len(args) - 1: 4}
    return pl.pallas_call(
        functools.partial(_in_kernel, n_ctx_tiles=n_ctx_tiles, tn=tn),
        grid=(n_tok // tm,),
        in_specs=in_specs,
        out_specs=[
            pl.BlockSpec((tm, ATTN_W), lambda i: (i, 0)),
            pl.BlockSpec((tm, 2 * KV_W), lambda i: (i, 0)),
            pl.BlockSpec((tm, n_cols - ATTN_W - 2 * KV_W), lambda i: (i, 0)),
            nkv_spec,
            nkv_spec,
            pl.BlockSpec((wo_rows, d), lambda i: (i, 0)),
        ],
        out_shape=[
            jax.ShapeDtypeStruct((n_tok, ATTN_W), BF16),
            jax.ShapeDtypeStruct((n_tok, 2 * KV_W), F32),
            jax.ShapeDtypeStruct((n_tok, n_cols - ATTN_W - 2 * KV_W), F32),
            jax.ShapeDtypeStruct(nkv_shape, F32),
            jax.ShapeDtypeStruct(nkv_shape, F32),
            jax.ShapeDtypeStruct(w_out.shape[1:], BF16),
        ],
        input_output_aliases=aliases,
        compiler_params=_params(("arbitrary",), 56),
        name="in_projection",
    )(*args)


def _softmax_pv(q, keys, vals, scale):
    dn = (((1,), (1,)), ((), ()))
    ss = [lax.dot_general(q, k, dn, preferred_element_type=F32) for k in keys]
    m = ss[0].max(axis=-1, keepdims=True)
    for s in ss[1:]:
        m = jnp.maximum(m, s.max(axis=-1, keepdims=True))
    es = [jnp.exp2((s - m) * (scale * LOG2E)) for s in ss]
    den = es[0].sum(axis=-1, keepdims=True)
    for e in es[1:]:
        den = den + e.sum(axis=-1, keepdims=True)
    o = jnp.dot(es[0].astype(BF16), vals[0], preferred_element_type=F32)
    for e, v in zip(es[1:], vals[1:]):
        o = o + jnp.dot(e.astype(BF16), v, preferred_element_type=F32)
    return o / den


def _attn_kernel(q_ref, kc_ref, vc_ref, ks_ref, vs_ref, ck_ref, cv_ref, o_ref, *, n_ctx_steps):
    scale = HEAD_DIM ** -0.5
    is_ctx = pl.program_id(0) < n_ctx_steps

    def attend(key_refs, val_refs):
        keys = [r[...].astype(BF16) for r in key_refs]
        vals = [r[...].astype(BF16) for r in val_refs]
        for g in range(Q_PER_KV):
            sl = slice(g * HEAD_DIM, (g + 1) * HEAD_DIM)
            o_ref[:, sl] = _softmax_pv(q_ref[:, sl], keys, vals, scale).astype(BF16)

    pl.when(is_ctx)(lambda: attend([kc_ref], [vc_ref]))
    pl.when(jnp.logical_not(is_ctx))(lambda: attend([ks_ref, ck_ref], [vs_ref, cv_ref]))


def _attention(q, kv, cache_k, cache_v, layer, *, n_ctx, seq_c, seq_s):
    gw = Q_PER_KV * HEAD_DIM
    tq = seq_c
    n_tok = q.shape[0]
    n_cseq = n_ctx // seq_c
    n_sseq = (n_tok - n_ctx) // seq_s
    past = cache_k.shape[2]
    n_qt = seq_s // tq
    n_ctx_steps = n_cseq * N_KV_HEADS
    n_smp_steps = n_sseq * N_KV_HEADS * n_qt
    q0 = n_ctx // tq
    s0 = n_ctx // seq_s
    assert n_ctx % seq_s == 0

    def ctx_bh(s):
        s = jnp.minimum(s, n_ctx_steps - 1)
        return s // N_KV_HEADS, s % N_KV_HEADS

    def smp_bht(s):
        r = jnp.maximum(s - n_ctx_steps, 0)
        return r // (N_KV_HEADS * n_qt), (r // n_qt) % N_KV_HEADS, r % n_qt

    def q_map(s):
        cb, ch = ctx_bh(s)
        sb, sh, st = smp_bht(s)
        is_ctx = s < n_ctx_steps
        return jnp.where(is_ctx, cb, q0 + sb * n_qt + st), jnp.where(is_ctx, ch, sh)

    return pl.pallas_call(
        functools.partial(_attn_kernel, n_ctx_steps=n_ctx_steps),
        grid=(n_ctx_steps + n_smp_steps,),
        in_specs=[
            pl.BlockSpec((tq, gw), q_map),
            pl.BlockSpec((seq_c, HEAD_DIM), lambda s: ctx_bh(s)),
            pl.BlockSpec((seq_c, HEAD_DIM), lambda s: (ctx_bh(s)[0], N_KV_HEADS + ctx_bh(s)[1])),
            pl.BlockSpec((seq_s, HEAD_DIM), lambda s: (s0 + smp_bht(s)[0], smp_bht(s)[1])),
            pl.BlockSpec((seq_s, HEAD_DIM), lambda s: (s0 + smp_bht(s)[0], N_KV_HEADS + smp_bht(s)[1])),
            pl.BlockSpec((None, None, past, HEAD_DIM), lambda s: (smp_bht(s)[0], layer, 0, smp_bht(s)[1])),
            pl.BlockSpec((None, None, past, HEAD_DIM), lambda s: (smp_bht(s)[0], layer, 0, smp_bht(s)[1])),
        ],
        out_specs=pl.BlockSpec((tq, gw), q_map),
        out_shape=jax.ShapeDtypeStruct((n_tok, ATTN_W), BF16),
        compiler_params=_params(("arbitrary",), 32),
        name="attention",
    )(q, kv, kv, kv, kv, cache_k, cache_v)


def _lane_major(a):
    depth, rows, width = a.shape
    return a.reshape(depth, rows, width // LANES, LANES).transpose(0, 2, 1, 3)


def _glu(a, b):
    return a * jax.nn.sigmoid(b)


def _conv_lane_tile(c, u1_ref, u2_ref, p1_ref, p2_ref, n1_ref, n2_ref, cw_ref, cb_ref, pad_ref, cvf_ref,
                    anchor, *, tile, n_ctx_tiles, tiles_per_seq):
    n_sub = pad_ref.shape[0]
    sub, halo = CONV_TILE, CONV_HALO
    is_smp = tile >= n_ctx_tiles
    pos = (tile - n_ctx_tiles) % tiles_per_seq
    has_prev = jnp.logical_and(is_smp, pos != 0)
    has_next = jnp.logical_and(is_smp, pos != tiles_per_seq - 1)

    sl = slice(c * LANES, (c + 1) * LANES)
    g = _glu(u1_ref[:, sl], u2_ref[:, sl])
    if anchor is not None:
        g = g + anchor
    for s in range(n_sub):
        if s == 0:
            lo = jnp.where(has_prev, _glu(p1_ref[:, sl], p2_ref[:, sl]), 0.0)
        else:
            lo = jnp.where(is_smp, g[s * sub - halo:s * sub], 0.0)
        if s == n_sub - 1:
            hi = jnp.where(has_next, _glu(n1_ref[:, sl], n2_ref[:, sl]), 0.0)
        else:
            hi = jnp.where(is_smp, g[(s + 1) * sub:(s + 1) * sub + halo], 0.0)
        pad_ref[s, c, 0:halo, :] = lo
        pad_ref[s, c, halo:halo + sub, :] = g[s * sub:(s + 1) * sub]
        pad_ref[s, c, halo + sub:, :] = hi

    for s in range(n_sub):
        for r0 in range(0, sub, CONV_CHUNK):
            acc = jnp.zeros((CONV_CHUNK, LANES), F32)
            for k in range(CONV_KERNEL):
                start = r0 + halo - CONV_PAD + k
                acc = acc + pad_ref[s, c, start:start + CONV_CHUNK, :] * cw_ref[c, k:k + 1, :]
            cvf_ref[c, s * sub + r0:s * sub + r0 + CONV_CHUNK, :] = acc + cb_ref[c]


def _conv_finish(cvf_ref, lg_ref, lb_ref, cv_ref):
    n_lt = cvf_ref.shape[0]
    width = n_lt * LANES
    tot = cvf_ref[0]
    for c in range(1, n_lt):
        tot = tot + cvf_ref[c]
    mu = tot.sum(axis=-1, keepdims=True) / width
    sq = jnp.zeros_like(tot)
    for c in range(n_lt):
        d = cvf_ref[c] - mu
        sq = sq + d * d
    rstd = lax.rsqrt(sq.sum(axis=-1, keepdims=True) / width + EPS)
    for c in range(n_lt):
        z = (cvf_ref[c] - mu) * rstd * lg_ref[c] + lb_ref[c]
        cv_ref[:, c * LANES:(c + 1) * LANES] = (z * jax.nn.sigmoid(z)).astype(BF16)


def _out_chunk(c, n_chunks, a_ref, cv_ref, x_ref, mod_ref, w_ref, xo_ref):
    ka = a_ref.shape[1]
    ncols = xo_ref.shape[1] // n_chunks
    cols = slice(c * ncols, (c + 1) * ncols)
    mix = jnp.dot(a_ref[...], w_ref[0:ka, cols], preferred_element_type=F32)
    mix = mix + jnp.dot(cv_ref[...], w_ref[ka:, cols], preferred_element_type=F32)
    xc = x_ref[:, cols] + mod_ref[2:3, cols] * mix
    xo_ref[:, cols] = xc
    return jnp.sum(xc * xc, axis=-1, keepdims=True), xc[0:1, 0:LANES]


def _zero_after(v):
    bits = lax.bitcast_convert_type(v, jnp.uint32)
    return lax.bitcast_convert_type((bits >> 16) >> 16, F32)


def _out_finish(ssq, xo_ref, mod_ref, g2_ref, h_ref):
    x = xo_ref[...]
    y = x * lax.rsqrt(ssq / x.shape[1] + EPS) * g2_ref[...]
    h_ref[...] = (y * (1.0 + mod_ref[4:5, :]) + mod_ref[3:4, :]).astype(BF16)


def _mix_kernel(u1_ref, u2_ref, p1_ref, p2_ref, n1_ref, n2_ref, cw_ref, cb_ref, lg_ref, lb_ref,
                a_ref, x_ref, mod_ref, g2_ref, w_ref, xo_ref, h_ref, pad_ref, cvf_ref, cur_ref, nxt_ref,
                *, n_ctx_tiles, tiles_per_seq):
    i = pl.program_id(0)
    n_tiles = pl.num_programs(0) - 1
    n_lt = cvf_ref.shape[0]

    def step(do_conv, do_out, cv_dst):
        ssq = None
        for c in range(n_lt):
            anchor = None
            if do_out:
                part, probe = _out_chunk(c, n_lt, a_ref, cur_ref, x_ref, mod_ref, w_ref, xo_ref)
                ssq = part if ssq is None else ssq + part
                anchor = _zero_after(probe)
            if do_conv:
                _conv_lane_tile(c, u1_ref, u2_ref, p1_ref, p2_ref, n1_ref, n2_ref, cw_ref, cb_ref, pad_ref,
                                cvf_ref, anchor, tile=i, n_ctx_tiles=n_ctx_tiles, tiles_per_seq=tiles_per_seq)
        if do_conv:
            _conv_finish(cvf_ref, lg_ref, lb_ref, cv_dst)
        if do_out:
            _out_finish(ssq, xo_ref, mod_ref, g2_ref, h_ref)

    @pl.when(i == 0)
    def _():
        step(True, False, cur_ref)

    @pl.when(jnp.logical_and(i > 0, i < n_tiles))
    def _():
        step(True, True, nxt_ref)
        cur_ref[...] = nxt_ref[...]

    @pl.when(i == n_tiles)
    def _():
        step(False, True, None)


def _mix_projection(u, attn, x, mod, g2, w_out, conv_w, conv_b, ln_g, ln_b, layer, *, n_ctx, seq_c, seq_s, tm=512):
    n_tok, d = x.shape
    width = u.shape[1] // 2
    n_lt = width // LANES
    ka = attn.shape[1]
    assert seq_c == CONV_TILE and tm % CONV_TILE == 0 and seq_s % tm == 0 and n_ctx % tm == 0
    n_tiles = n_tok // tm
    n_ctx_tiles = n_ctx // tm
    tiles_per_seq = seq_s // tm
    hb = tm // CONV_HALO
    last_hb = n_tok // CONV_HALO - 1
    mod_row = _mod_row_fn(n_ctx, seq_s, tm)
    conv_tile = lambda i: jnp.minimum(i, n_tiles - 1)
    out_tile = lambda i: jnp.maximum(i - 1, 0)
    prev_map = lambda c: (lambda i: (jnp.maximum(conv_tile(i) * hb - 1, 0), c))
    next_map = lambda c: (lambda i: (jnp.minimum((conv_tile(i) + 1) * hb, last_hb), c))
    vec_spec = pl.BlockSpec((None, n_lt, 1, LANES), lambda i: (layer, 0, 0, 0))
    kern = functools.partial(_mix_kernel, n_ctx_tiles=n_ctx_tiles, tiles_per_seq=tiles_per_seq)
    return pl.pallas_call(
        kern,
        grid=(n_tiles + 1,),
        in_specs=[
            pl.BlockSpec((tm, width), lambda i: (conv_tile(i), 0)),
            pl.BlockSpec((tm, width), lambda i: (conv_tile(i), 1)),
            pl.BlockSpec((CONV_HALO, width), prev_map(0)),
            pl.BlockSpec((CONV_HALO, width), prev_map(1)),
            pl.BlockSpec((CONV_HALO, width), next_map(0)),
            pl.BlockSpec((CONV_HALO, width), next_map(1)),
            pl.BlockSpec((None, n_lt, CONV_KERNEL, LANES), lambda i: (layer, 0, 0, 0)),
            vec_spec, vec_spec, vec_spec,
            pl.BlockSpec((tm, ka), lambda i: (out_tile(i), 0)),
            pl.BlockSpec((tm, d), lambda i: (out_tile(i), 0)),
            pl.BlockSpec((None, None, N_MOD, d), lambda i: (layer, mod_row(out_tile(i)), 0, 0)),
            pl.BlockSpec((None, 1, d), lambda i: (layer, 0, 0)),
            pl.BlockSpec((ka + width, d), lambda i: (0, 0), pipeline_mode=RESIDENT),
        ],
        out_specs=[
            pl.BlockSpec((tm, d), lambda i: (out_tile(i), 0)),
            pl.BlockSpec((tm, d), lambda i: (out_tile(i), 0)),
        ],
        out_shape=[
            jax.ShapeDtypeStruct((n_tok, d), F32),
            jax.ShapeDtypeStruct((n_tok, d), BF16),
        ],
        scratch_shapes=[
            pltpu.VMEM((tm // CONV_TILE, n_lt, CONV_TILE + 2 * CONV_HALO, LANES), F32),
            pltpu.VMEM((n_lt, tm, LANES), F32),
            pltpu.VMEM((tm, width), BF16),
            pltpu.VMEM((tm, width), BF16),
        ],
        compiler_params=_params(("arbitrary",), 56),
        name="mix_projection",
    )(u, u, u, u, u, u, conv_w, conv_b, ln_g, ln_b, attn, x, mod, g2, w_out)


def _ff1_kernel(h_ref, w_ref, w2_ref, a_ref, w2b_ref, *, rows):
    w2b_ref[...] = w2_ref[...].astype(BF16)
    w = w_ref[...].astype(BF16)
    for r0 in range(0, a_ref.shape[0], rows):
        a = jnp.dot(h_ref[r0:r0 + rows, :], w, preferred_element_type=F32)
        a_ref[r0:r0 + rows, :] = jnp.square(jnp.maximum(a, 0.0)).astype(BF16)


def _ff1(h, w1, w2, layer, *, tm=2048, tn=1024, rows=1024):
    n_tok, d = h.shape
    d_ff = w1.shape[2]
    n_m = n_tok // tm
    n_steps = (d_ff // tn) * n_m
    w2_rows = d_ff // n_steps
    assert d_ff % n_steps == 0
    return pl.pallas_call(
        functools.partial(_ff1_kernel, rows=rows),
        grid=(d_ff // tn, n_m),
        in_specs=[
            pl.BlockSpec((tm, d), lambda n, m: (m, 0)),
            pl.BlockSpec((None, d, tn), lambda n, m: (layer, 0, n)),
            pl.BlockSpec((None, w2_rows, d), lambda n, m: (layer, n * n_m + m, 0)),
        ],
        out_specs=[
            pl.BlockSpec((tm, tn), lambda n, m: (m, n)),
            pl.BlockSpec((w2_rows, d), lambda n, m: (n * n_m + m, 0)),
        ],
        out_shape=[
            jax.ShapeDtypeStruct((n_tok, d_ff), BF16),
            jax.ShapeDtypeStruct((d_ff, d), BF16),
        ],
        compiler_params=_params(("arbitrary", "arbitrary"), 56),
        name="ff1",
    )(h, w1, w2)


def _ff2_kernel(a_ref, w_ref, x_ref, mod_ref, *rest):
    o_ref = rest[len(rest) // 2]
    if len(rest) == 3:
        rest[2][...] = rest[0][...].astype(BF16)
    p = jnp.dot(a_ref[...], w_ref[...], preferred_element_type=F32)
    o_ref[...] = x_ref[...] + mod_ref[5:6, :] * p


def _ff2(a, w2, x, mod, layer, tile0, n_tiles, w_in_next=None, *, n_ctx, seq_s, tm=1024, tn=256):
    d_ff, d = w2.shape
    n_n = d // tn
    mod_row = _mod_row_fn(n_ctx, seq_s, tm)
    in_specs = [
        pl.BlockSpec((tm, d_ff), lambda m, n: (tile0 + m, 0)),
        pl.BlockSpec((d_ff, tn), lambda m, n: (0, n)),
        pl.BlockSpec((tm, tn), lambda m, n: (tile0 + m, n)),
        pl.BlockSpec((None, None, N_MOD, tn), lambda m, n: (layer, mod_row(tile0 + m), 0, n)),
    ]
    out_specs = [pl.BlockSpec((tm, tn), lambda m, n: (m, n))]
    out_shape = [jax.ShapeDtypeStruct((n_tiles * tm, d), F32)]
    args = [a, w2, x, mod]
    if w_in_next is not None:
        _, wi_rows, wi_cols = w_in_next.shape
        slab = wi_rows // (n_tiles * n_n)
        assert wi_rows % (n_tiles * n_n) == 0
        in_specs.append(pl.BlockSpec((None, slab, wi_cols), lambda m, n: (layer + 1, m * n_n + n, 0)))
        out_specs.append(pl.BlockSpec((slab, wi_cols), lambda m, n: (m * n_n + n, 0)))
        out_shape.append(jax.ShapeDtypeStruct((wi_rows, wi_cols), BF16))
        args.append(w_in_next)
    return pl.pallas_call(
        _ff2_kernel,
        grid=(n_tiles, n_n),
        in_specs=in_specs,
        out_specs=out_specs,
        out_shape=out_shape,
        compiler_params=_params(("arbitrary", "arbitrary"), 52),
        name="ff2",
    )(*args)


def _rotary_tables(n_tokens, n_identity):
    n_rows = n_tokens // GRID_W
    row = jnp.repeat(jnp.arange(n_rows, dtype=jnp.int32), GRID_W).astype(F32)
    col = jnp.tile(jnp.arange(GRID_W, dtype=jnp.int32), n_rows).astype(F32)
    n_pairs_axis = HEAD_DIM // 4
    freqs = ROPE_THETA ** (-jnp.arange(n_pairs_axis, dtype=F32) / n_pairs_axis)
    ang = jnp.concatenate([row[:, None] * freqs, col[:, None] * freqs], axis=-1)
    cos, sin = jnp.cos(ang), jnp.sin(ang)
    cos2 = jnp.concatenate([cos, cos], axis=-1)
    sin2 = jnp.concatenate([-sin, sin], axis=-1)
    return (jnp.concatenate([cos2, jnp.ones((n_identity, HEAD_DIM), F32)], axis=0),
            jnp.concatenate([sin2, jnp.zeros((n_identity, HEAD_DIM), F32)], axis=0))


def kernel(x_prompt, x_sample, cache_k, cache_v, c, c_ctx, norm1_g, w_mod, b_mod, w_in, q_norm_g, k_norm_g,
           conv_w, conv_b, conv_norm_g, conv_norm_b, w_out, norm2_g, w_ff1, w_ff2):
    n_cseq, seq_c, d = x_prompt.shape
    n_sseq, seq_s, _ = x_sample.shape
    depth = w_in.shape[0]
    n_ctx = n_cseq * seq_c
    n_smp = n_sseq * seq_s
    in_tm = 512
    ff_tm = 1024

    x = jnp.concatenate([x_prompt.reshape(n_ctx, d), x_sample.reshape(n_smp, d)], axis=0)
    cond8 = jnp.concatenate([c_ctx[None, :], c, jnp.zeros((8 - 1 - n_sseq, d), F32)], axis=0)
    mod = _modulation(cond8, w_mod, b_mod).reshape(depth, 8, N_MOD, d)
    cos_ext, sin_ext = _rotary_tables(seq_s, in_tm)
    ck = cache_k.reshape(n_sseq, depth, cache_k.shape[2], KV_W)
    cvv = cache_v.reshape(n_sseq, depth, cache_v.shape[2], KV_W)

    g1, g2 = norm1_g[:, None, :], norm2_g[:, None, :]
    qg, kg = q_norm_g[:, None, :], k_norm_g[:, None, :]
    conv_w_lm = _lane_major(conv_w)
    conv_b_lm, ln_g_lm, ln_b_lm = (_lane_major(a[:, None, :]) for a in (conv_b, conv_norm_g, conv_norm_b))
    w_in_b = w_in[0].astype(BF16)
    dims = dict(n_ctx=n_ctx, seq_s=seq_s)

    new_kv = [jnp.zeros((n_cseq, depth, seq_c, KV_W), F32) for _ in range(2)]
    for l in range(depth):
        q, kv, u, *new_kv, w_out_b = _in_projection(x, mod, g1, w_in_b, qg, kg, cos_ext, sin_ext, w_out, new_kv, l,
                                                    seq_c=seq_c, tm=in_tm, **dims)
        attn = _attention(q, kv, ck, cvv, l, n_ctx=n_ctx, seq_c=seq_c, seq_s=seq_s)
        x, h2 = _mix_projection(u, attn, x, mod, g2, w_out_b, conv_w_lm, conv_b_lm, ln_g_lm, ln_b_lm, l,
                                seq_c=seq_c, **dims)
        a, w_ff2_b = _ff1(h2, w_ff1, w_ff2, l)
        if l < depth - 1:
            x, w_in_b = _ff2(a, w_ff2_b, x, mod, l, 0, (n_ctx + n_smp) // ff_tm, w_in, tm=ff_tm, **dims)
        else:
            y_prompt, = _ff2(a, w_ff2_b, x, mod, l, 0, n_ctx // ff_tm, tm=ff_tm, **dims)
            y_sample, = _ff2(a, w_ff2_b, x, mod, l, n_ctx // ff_tm, n_smp // ff_tm, tm=ff_tm, **dims)

    new_k, new_v = new_kv
    kv_shape = (n_cseq, depth, seq_c, N_KV_HEADS, HEAD_DIM)
    return (y_prompt.reshape(n_cseq, seq_c, d), y_sample.reshape(n_sseq, seq_s, d),
            new_k.reshape(kv_shape), new_v.reshape(kv_shape))
```
